```python
import math
import jax, jax.numpy as jnp
from jax import lax
import numpy as np

D_MODEL = 1024
BATCH = 8
SEQ = 2048
DEPTH = 4
DEC_BATCH = 128
DEC_SEQ = 4
PAST_LEN = 16384
PAGE_SIZE = 128

N_EVEN = (DEPTH + 1) // 2
N_ODD = DEPTH // 2
A_WIDTH = D_MODEL // 2
A_GROUP = 16
A_GROUPS = A_WIDTH // A_GROUP
A_STATE = 64
S5_MIN_STEP = 1e-3
S5_MAX_STEP = 1e-1
B_WIDTH = D_MODEL // 2
B_EXPAND = 128
B_HEADS = B_WIDTH // B_EXPAND
B_KDIM = B_EXPAND
B_VDIM = B_WIDTH // B_HEADS
B_CHUNK = 16
EVEN_SPLITS = tuple(int(s) for s in np.cumsum([A_WIDTH, A_WIDTH, B_WIDTH, B_WIDTH, B_WIDTH]))
EVEN_IN = 2 * A_WIDTH + 4 * B_WIDTH
C_HEAD = 64
C_HEADS = D_MODEL // C_HEAD
C_DECAY_LORA = max(32, int(round(1.8 * D_MODEL ** 0.5 / 32)) * 32)
C_AAA_LORA = max(32, int(round(1.8 * D_MODEL ** 0.5 / 32)) * 32)
C_MV_LORA = max(32, int(round(1.3 * D_MODEL ** 0.5 / 32)) * 32)
C_GATE_LORA = max(32, int(round(0.6 * D_MODEL ** 0.8 / 32)) * 32)
C_VRES_LAYERS = max(N_ODD - 1, 0)
DECAY_SCALE = math.exp(-0.5)
RMS_EPS = 1e-6
GN_EPS = 64e-5

kernel_name = 'hybrid_s5_hgrn2_rwkv7_step'


def _rmsnorm(x, w):
    x32 = x.astype(jnp.float32)
    return x32 * lax.rsqrt(jnp.mean(x32 * x32, axis=-1, keepdims=True) + RMS_EPS) * w.astype(jnp.float32)


def _complex_affine_combine(e1, e2):
    a1r, a1i, b1r, b1i = e1
    a2r, a2i, b2r, b2i = e2
    return (a2r * a1r - a2i * a1i,
            a2r * a1i + a2i * a1r,
            a2r * b1r - a2i * b1i + b2r,
            a2r * b1i + a2i * b1r + b2i)


def _s5(u, lam_re, lam_im, log_step, b_re, b_im, c_re, c_im, d, h0_re, h0_im):
    f32 = jnp.float32
    lr = jnp.minimum(lam_re.astype(f32), -1e-4)
    li = lam_im.astype(f32)
    step = jnp.exp(log_step.astype(f32))[:, None]
    mag = jnp.exp(lr * step)
    ab_re = mag * jnp.cos(li * step)
    ab_im = mag * jnp.sin(li * step)
    den = lr * lr + li * li
    nr = ab_re - 1.0
    cr = (nr * lr + ab_im * li) / den
    ci = (ab_im * lr - nr * li) / den
    b_re = b_re.astype(f32)
    b_im = b_im.astype(f32)
    bb_re = cr[..., None] * b_re - ci[..., None] * b_im
    bb_im = cr[..., None] * b_im + ci[..., None] * b_re
    bu_re = jnp.einsum('btgh,gph->btgp', u, bb_re)
    bu_im = jnp.einsum('btgh,gph->btgp', u, bb_im)
    h0_re = h0_re.astype(f32)
    h0_im = h0_im.astype(f32)
    bu_re = bu_re.at[:, 0].add(ab_re * h0_re - ab_im * h0_im)
    bu_im = bu_im.at[:, 0].add(ab_re * h0_im + ab_im * h0_re)
    a_re = jnp.broadcast_to(ab_re, bu_re.shape)
    a_im = jnp.broadcast_to(ab_im, bu_im.shape)
    _, _, h_re, h_im = lax.associative_scan(_complex_affine_combine, (a_re, a_im, bu_re, bu_im), axis=1)
    y = (jnp.einsum('btgp,ghp->btgh', h_re, c_re.astype(f32))
         - jnp.einsum('btgp,ghp->btgh', h_im, c_im.astype(f32))
         + d.astype(f32) * u)
    return y, h_re[:, -1], h_im[:, -1]


def _hgrn2_chunked(q, k, v, log_f, s0):
    bsz, t, h, _ = q.shape
    dv = v.shape[-1]
    c = math.gcd(t, B_CHUNK)
    n = t // c

    def to_chunks(a):
        return a.reshape(bsz, n, c, h, a.shape[-1]).transpose(1, 0, 3, 2, 4)

    causal = jnp.tril(jnp.ones((c, c), dtype=bool))

    def step(s, blk):
        qb, kb, vb, gb = blk
        b = jnp.cumsum(gb, axis=2)
        b_last = b[:, :, -1:]
        q_in = qb * jnp.exp(b)
        k_in = kb * jnp.exp(-b)
        k_end = kb * jnp.exp(b_last - b)
        att = jnp.where(causal, jnp.einsum('bhtk,bhsk->bhts', q_in, k_in), 0.0)
        o = jnp.einsum('bhts,bhsv->bhtv', att, vb) + jnp.einsum('bhtk,bhkv->bhtv', q_in, s)
        s_new = jnp.exp(b_last[:, :, 0])[..., None] * s + jnp.einsum('bhsk,bhsv->bhkv', k_end, vb)
        return s_new, o

    s_fin, oc = lax.scan(step, s0.astype(jnp.float32),
                         (to_chunks(q), to_chunks(k), to_chunks(v), to_chunks(log_f)))
    return oc.transpose(1, 0, 3, 2, 4).reshape(bsz, t, h, dv), s_fin


def _rwkv7_scan(r, decay, k, v, kk, a, s0):
    def step(s, inp):
        r_t, w_t, k_t, v_t, kk_t, a_t = inp
        s = (s * w_t[:, :, None, :]
             - jnp.einsum('bhvk,bhk->bhv', s, kk_t)[..., None] * (kk_t * a_t)[:, :, None, :]
             + v_t[..., None] * k_t[:, :, None, :])
        return s, jnp.einsum('bhvk,bhk->bhv', s, r_t)

    xs = tuple(jnp.swapaxes(z, 0, 1) for z in (r, decay, k, v, kk, a))
    s_fin, y = lax.scan(step, s0.astype(jnp.float32), xs)
    return jnp.swapaxes(y, 0, 1), s_fin


def _even_layer(xn, j, w, h0_re, h0_im, s0):
    bsz, t, _ = xn.shape
    proj = xn @ w['even_w_in'][j]
    u, z_a, q, f, i, z_b = jnp.split(proj, EVEN_SPLITS, axis=-1)
    y, h_re, h_im = _s5(u.reshape(bsz, t, A_GROUPS, A_GROUP),
                        w['ssm_lambda_re'][j], w['ssm_lambda_im'][j], w['ssm_log_step'][j],
                        w['ssm_b_re'][j], w['ssm_b_im'][j], w['ssm_c_re'][j], w['ssm_c_im'][j],
                        w['ssm_d'][j], h0_re, h0_im)
    y = jax.nn.gelu(y.reshape(bsz, t, A_WIDTH))
    y = y * jax.nn.sigmoid(y @ w['ssm_glu_w'][j] + w['ssm_glu_b'][j])
    out_a = y * jax.nn.silu(z_a)
    heads = lambda a: a.reshape(bsz, t, B_HEADS, -1)
    qh = heads(jax.nn.silu(q))
    fh = heads(f).astype(jnp.float32)
    ih = heads(i)
    lbs = jax.nn.softmax(w['hgrn_lower_bounds'].astype(jnp.float32), axis=0)
    lbs = jnp.cumsum(lbs, axis=0) - lbs[0]
    if j == 0:
        log_f = jax.nn.log_sigmoid(fh)
        kh = jax.nn.sigmoid(-fh)
    else:
        lb = lbs[j].reshape(B_HEADS, B_KDIM)
        log_f = jnp.log(lb + (1.0 - lb) * jax.nn.sigmoid(fh))
        kh = (1.0 - lb) * jax.nn.sigmoid(-fh)
    o, s_fin = _hgrn2_chunked(qh, kh, ih, log_f, s0)
    o = o * lax.rsqrt(jnp.mean(o * o, axis=-1, keepdims=True) + RMS_EPS) * w['hgrn_norm_w'][j]
    out_b = o.reshape(bsz, t, B_WIDTH) * jax.nn.silu(z_b)
    out = jnp.concatenate([out_a, out_b], axis=-1) @ w['even_w_out'][j]
    return out, h_re, h_im, s_fin


def _odd_layer(xn, j, w, shift_prev, s0, v_first):
    bsz, t, _ = xn.shape
    x_prev = jnp.concatenate([shift_prev.astype(jnp.float32)[:, None], xn[:, :-1]], axis=1)
    xx = x_prev - xn
    mix = w['rw_mix'][j]
    xr, xw, xk, xv, xa, xg = (xn + xx * mix[m] for m in range(6))
    wp = w['rw_w_rkvz'][j]
    r = xr @ wp[0]
    k = xk @ wp[1]
    v = xv @ wp[2]
    z = xg @ wp[3]
    decay = jnp.exp(-DECAY_SCALE * jax.nn.sigmoid(
        (w['rw_w0'][j] + jnp.tanh(xw @ w['rw_w1'][j]) @ w['rw_w2'][j]).astype(jnp.float32)))
    if v_first is None:
        v_first = v
    else:
        jv = j - 1
        v = v + (v_first - v) * jax.nn.sigmoid(w['rw_v0'][jv] + (xv @ w['rw_v1'][jv]) @ w['rw_v2'][jv])
    a = jax.nn.sigmoid(w['rw_a0'][j] + (xa @ w['rw_a1'][j]) @ w['rw_a2'][j])
    g = jax.nn.sigmoid(xg @ w['rw_g1'][j]) @ w['rw_g2'][j]
    heads = lambda a_: a_.reshape(bsz, t, C_HEADS, C_HEAD)
    kk = heads(k * w['rw_k_k'][j])
    kk = kk / jnp.maximum(jnp.sqrt(jnp.sum(kk * kk, axis=-1, keepdims=True)), 1e-12)
    k = k * (1.0 + (a - 1.0) * w['rw_k_a'][j])
    rh, kh, vh, ah, dh = heads(r), heads(k), heads(v), heads(a), heads(decay)
    y, s_fin = _rwkv7_scan(rh, dh, kh, vh, kk, ah, s0)
    mu = jnp.mean(y, axis=-1, keepdims=True)
    var = jnp.mean((y - mu) ** 2, axis=-1, keepdims=True)
    y = ((y - mu) * lax.rsqrt(var + GN_EPS)).reshape(bsz, t, D_MODEL) * w['rw_ln_w'][j] + w['rw_ln_b'][j]
    bonus = (jnp.sum(rh * kh * w['rw_r_k'][j], axis=-1, keepdims=True) * vh).reshape(bsz, t, D_MODEL)
    out = (y + bonus) * g * jax.nn.silu(z)
    return out @ w['rw_w_o'][j], s_fin, xn[:, -1], v_first


def _trunk(x, ssm_re, ssm_im, hgrn, wkv, shift, w):
    h = x.astype(jnp.float32)
    n_re, n_im, n_hg, n_wkv, n_sh = [], [], [], [], []
    v_first = None
    for layer in range(DEPTH):
        xn = _rmsnorm(h, w['norm_w'][layer])
        j = layer // 2
        if layer % 2 == 0:
            out, hr, hi, sb = _even_layer(xn, j, w, ssm_re[j], ssm_im[j], hgrn[j])
            n_re.append(hr)
            n_im.append(hi)
            n_hg.append(sb)
        else:
            out, sw, last, v_first = _odd_layer(xn, j, w, shift[j], wkv[j], v_first)
            n_wkv.append(sw)
            n_sh.append(last)
        h = h + out
    y = _rmsnorm(h, w['final_norm_w']).astype(x.dtype)
    return y, jnp.stack(n_re), jnp.stack(n_im), jnp.stack(n_hg), jnp.stack(n_wkv), jnp.stack(n_sh)


def setup_inputs(seed: int = 0) -> dict:
    key = jax.random.key(seed)
    ks = iter(jax.random.split(key, 64))
    f32 = jnp.float32
    D = D_MODEL
    G, H, P = A_GROUPS, A_GROUP, A_STATE

    def nrm(shape, scale=1.0):
        return scale * jax.random.normal(next(ks), shape, f32)

    def unif(shape, lo, hi):
        return jax.random.uniform(next(ks), shape, f32, lo, hi)

    return {
        'x_prompt': nrm((BATCH, SEQ, D)),
        'x_sample': nrm((DEC_BATCH, DEC_SEQ, D)),
        'state_ssm_re': nrm((N_EVEN, DEC_BATCH, G, P)),
        'state_ssm_im': nrm((N_EVEN, DEC_BATCH, G, P)),
        'state_hgrn': nrm((N_EVEN, DEC_BATCH, B_HEADS, B_KDIM, B_VDIM)),
        'state_wkv': nrm((N_ODD, DEC_BATCH, C_HEADS, C_HEAD, C_HEAD)),
        'state_shift': nrm((N_ODD, DEC_BATCH, D)),
        'norm_w': 1.0 + nrm((DEPTH, D), 0.02),
        'final_norm_w': 1.0 + nrm((D,), 0.02),
        'even_w_in': nrm((N_EVEN, D, EVEN_IN), D ** -0.5),
        'even_w_out': nrm((N_EVEN, A_WIDTH + B_WIDTH, D), (A_WIDTH + B_WIDTH) ** -0.5),
        'ssm_lambda_re': -0.5 + nrm((N_EVEN, G, P), 0.01),
        'ssm_lambda_im': jnp.pi * jnp.arange(P, dtype=f32) + nrm((N_EVEN, G, P), 0.01),
        'ssm_log_step': unif((N_EVEN, G), math.log(S5_MIN_STEP), math.log(S5_MAX_STEP)),
        'ssm_b_re': nrm((N_EVEN, G, P, H), (2 * H) ** -0.5),
        'ssm_b_im': nrm((N_EVEN, G, P, H), (2 * H) ** -0.5),
        'ssm_c_re': nrm((N_EVEN, G, H, P), P ** -0.5),
        'ssm_c_im': nrm((N_EVEN, G, H, P), P ** -0.5),
        'ssm_d': nrm((N_EVEN, G, H)),
        'ssm_glu_w': nrm((N_EVEN, A_WIDTH, A_WIDTH), A_WIDTH ** -0.5),
        'ssm_glu_b': nrm((N_EVEN, A_WIDTH), 0.01),
        'hgrn_lower_bounds': nrm((N_EVEN, B_WIDTH), 0.1),
        'hgrn_norm_w': 1.0 + nrm((N_EVEN, B_VDIM), 0.02),
        'rw_mix': unif((N_ODD, 6, D), 0.0, 1.0),
        'rw_w_rkvz': nrm((N_ODD, 4, D, D), D ** -0.5),
        'rw_w0': unif((N_ODD, D), -6.0, 0.0),
        'rw_w1': nrm((N_ODD, D, C_DECAY_LORA), D ** -0.5),
        'rw_w2': nrm((N_ODD, C_DECAY_LORA, D), 0.1 * C_DECAY_LORA ** -0.5),
        'rw_a0': nrm((N_ODD, D), 0.1),
        'rw_a1': nrm((N_ODD, D, C_AAA_LORA), D ** -0.5),
        'rw_a2': nrm((N_ODD, C_AAA_LORA, D), 0.1 * C_AAA_LORA ** -0.5),
        'rw_v0': 1.0 + nrm((C_VRES_LAYERS, D), 0.1),
        'rw_v1': nrm((C_VRES_LAYERS, D, C_MV_LORA), D ** -0.5),
        'rw_v2': nrm((C_VRES_LAYERS, C_MV_LORA, D), 0.1 * C_MV_LORA ** -0.5),
        'rw_g1': nrm((N_ODD, D, C_GATE_LORA), D ** -0.5),
        'rw_g2': nrm((N_ODD, C_GATE_LORA, D), C_GATE_LORA ** -0.5),
        'rw_k_k': 0.85 + nrm((N_ODD, D), 0.02),
        'rw_k_a': 1.0 + nrm((N_ODD, D), 0.02),
        'rw_r_k': nrm((N_ODD, C_HEADS, C_HEAD), 0.1),
        'rw_ln_w': 1.0 + nrm((N_ODD, D), 0.02),
        'rw_ln_b': nrm((N_ODD, D), 0.01),
        'rw_w_o': nrm((N_ODD, D, D), D ** -0.5),
    }


def reference(x_prompt, x_sample, state_ssm_re, state_ssm_im, state_hgrn, state_wkv, state_shift,
              norm_w, final_norm_w, even_w_in, even_w_out, ssm_lambda_re, ssm_lambda_im, ssm_log_step,
              ssm_b_re, ssm_b_im, ssm_c_re, ssm_c_im, ssm_d, ssm_glu_w, ssm_glu_b,
              hgrn_lower_bounds, hgrn_norm_w, rw_mix, rw_w_rkvz, rw_w0, rw_w1, rw_w2,
              rw_a0, rw_a1, rw_a2, rw_v0, rw_v1, rw_v2, rw_g1, rw_g2, rw_k_k, rw_k_a, rw_r_k,
              rw_ln_w, rw_ln_b, rw_w_o):
    w = dict(norm_w=norm_w, final_norm_w=final_norm_w, even_w_in=even_w_in, even_w_out=even_w_out,
             ssm_lambda_re=ssm_lambda_re, ssm_lambda_im=ssm_lambda_im, ssm_log_step=ssm_log_step,
             ssm_b_re=ssm_b_re, ssm_b_im=ssm_b_im, ssm_c_re=ssm_c_re, ssm_c_im=ssm_c_im, ssm_d=ssm_d,
             ssm_glu_w=ssm_glu_w, ssm_glu_b=ssm_glu_b, hgrn_lower_bounds=hgrn_lower_bounds,
             hgrn_norm_w=hgrn_norm_w, rw_mix=rw_mix, rw_w_rkvz=rw_w_rkvz, rw_w0=rw_w0, rw_w1=rw_w1,
             rw_w2=rw_w2, rw_a0=rw_a0, rw_a1=rw_a1, rw_a2=rw_a2, rw_v0=rw_v0, rw_v1=rw_v1, rw_v2=rw_v2,
             rw_g1=rw_g1, rw_g2=rw_g2, rw_k_k=rw_k_k, rw_k_a=rw_k_a, rw_r_k=rw_r_k,
             rw_ln_w=rw_ln_w, rw_ln_b=rw_ln_b, rw_w_o=rw_w_o)
    f32 = jnp.float32
    bp = x_prompt.shape[0]
    z_re = jnp.zeros((N_EVEN, bp, A_GROUPS, A_STATE), f32)
    z_hg = jnp.zeros((N_EVEN, bp, B_HEADS, B_KDIM, B_VDIM), f32)
    z_wkv = jnp.zeros((N_ODD, bp, C_HEADS, C_HEAD, C_HEAD), f32)
    z_sh = jnp.zeros((N_ODD, bp, D_MODEL), f32)
    y_prompt, p_re, p_im, p_hg, p_wkv, p_sh = _trunk(x_prompt, z_re, z_re, z_hg, z_wkv, z_sh, w)
    y_sample, s_re, s_im, s_hg, s_wkv, s_sh = _trunk(
        x_sample, state_ssm_re, state_ssm_im, state_hgrn, state_wkv, state_shift, w)
    return (y_prompt, y_sample, p_re, p_im, p_hg, p_wkv, p_sh, s_re, s_im, s_hg, s_wkv, s_sh)
```

```python
import functools
import math

import jax
import jax.numpy as jnp
from jax import lax
from jax.experimental import pallas as pl
from jax.experimental.pallas import tpu as pltpu

F32 = jnp.float32
PREC = lax.Precision.HIGHEST

D_MODEL = 1024
A_WIDTH = 512
A_GROUP = 16
A_GROUPS = 32
A_STATE = 64
A_LANES = A_GROUPS * A_STATE
A_BLOCKS = 4
B_WIDTH = 512
B_HEADS = 4
B_DIM = 128
C_HEADS = 16
C_HEAD = 64
DECAY_SCALE = math.exp(-0.5)
RMS_EPS = 1e-6
GN_EPS = 64e-5
VMEM_LIMIT = 56 * 1024 * 1024
LANES = 128


def _dot(a, b):
    return jnp.dot(a, b, precision=PREC, preferred_element_type=F32)


def _bdot(a, b, contract):
    return lax.dot_general(a, b, (contract, ((0,), (0,))), precision=PREC, preferred_element_type=F32)


def _nn(a, b):
    return _bdot(a, b, ((2,), (1,)))


def _nt(a, b):
    return _bdot(a, b, ((2,), (2,)))


def _tn(a, b):
    return _bdot(a, b, ((1,), (1,)))


def _rms(x, w):
    return x * lax.rsqrt(jnp.mean(x * x, axis=-1, keepdims=True) + RMS_EPS) * w


def _sigmoid(x):
    return 1.0 / (1.0 + jnp.exp(-x))


def _silu(x):
    return x * _sigmoid(x)


def _params(sem):
    return pltpu.CompilerParams(dimension_semantics=sem, vmem_limit_bytes=VMEM_LIMIT)


def _tri_masks(c):
    row = lax.broadcasted_iota(jnp.int32, (c, c), 0)
    col = lax.broadcasted_iota(jnp.int32, (c, c), 1)
    return row >= col, row > col, row == col


def _split_heads(x, width):
    bb, _, d = x.shape
    return jnp.stack([x[i, :, h * width:(h + 1) * width] for i in range(bb) for h in range(d // width)], axis=0)


def _merge_heads(x, bb):
    h = x.shape[0] // bb
    return jnp.stack([jnp.concatenate([x[i * h + j] for j in range(h)], axis=-1) for i in range(bb)], axis=0)


def _cumsum_time(tri_f, x):
    return jnp.stack([_dot(tri_f, x[i]) for i in range(x.shape[0])], axis=0)


def _norm_matmul_kernel(x_ref, nw_ref, w_ref, o_ref):
    o_ref[...] = _dot(_rms(x_ref[...], nw_ref[...]), w_ref[...])


def _norm_matmul(x, nw, w, *, tm, tn):
    m, k = x.shape
    n = w.shape[1]
    return pl.pallas_call(
        _norm_matmul_kernel, grid=(m // tm, n // tn),
        in_specs=[pl.BlockSpec((tm, k), lambda i, j: (i, 0)),
                  pl.BlockSpec((1, k), lambda i, j: (0, 0)),
                  pl.BlockSpec((k, tn), lambda i, j: (0, j))],
        out_specs=pl.BlockSpec((tm, tn), lambda i, j: (i, j)),
        out_shape=jax.ShapeDtypeStruct((m, n), F32),
        compiler_params=_params(("parallel", "arbitrary")), name="norm_in_proj",
    )(x, nw, w)


def _proj_residual_kernel(*refs, n_x, final_norm):
    h_ref, x_refs, w_ref = refs[0], refs[1:1 + n_x], refs[1 + n_x]
    x = jnp.concatenate([r[...] for r in x_refs], axis=-1) if n_x > 1 else x_refs[0][...]
    h = h_ref[...] + _dot(x, w_ref[...])
    if final_norm:
        nw_ref, h_out, y_out = refs[2 + n_x:]
        y_out[...] = _rms(h, nw_ref[...])
    else:
        h_out = refs[2 + n_x]
    h_out[...] = h


def _proj_residual(h, xs, w, final_nw=None, *, tm):
    m, n = h.shape
    final_norm = final_nw is not None
    row = lambda width: pl.BlockSpec((tm, width), lambda i: (i, 0))
    in_specs = [row(n)] + [row(x.shape[1]) for x in xs] + [pl.BlockSpec(w.shape, lambda i: (0, 0))]
    args = [h, *xs, w]
    out_specs, out_shape = [row(n)], [jax.ShapeDtypeStruct((m, n), F32)]
    if final_norm:
        in_specs.append(pl.BlockSpec((1, n), lambda i: (0, 0)))
        args.append(final_nw)
        out_specs.append(row(n))
        out_shape.append(jax.ShapeDtypeStruct((m, n), F32))
    return pl.pallas_call(
        functools.partial(_proj_residual_kernel, n_x=len(xs), final_norm=final_norm), grid=(m // tm,),
        in_specs=in_specs, out_specs=out_specs, out_shape=out_shape,
        compiler_params=_params(("parallel",)), name="out_proj_residual",
    )(*args)


def _s5_kernel(u_ref, za_ref, h0r_ref, h0i_ref, ar_ref, ai_ref, bbr_ref, bbi_ref, cr_ref, ci_ref,
               d_ref, gw_ref, gb_ref, out_ref, hfr_ref, hfi_ref, hre, him, st_r, st_i,
               *, bsz, tc, n_steps):
    i = pl.program_id(0)

    n_tiles = A_LANES // LANES

    @pl.when(i == 0)
    def _():
        for cb in range(n_tiles):
            st_r[cb] = h0r_ref[:, cb * LANES:(cb + 1) * LANES]
            st_i[cb] = h0i_ref[:, cb * LANES:(cb + 1) * LANES]

    u2 = u_ref[...].reshape(bsz * tc, A_WIDTH)
    ub = A_WIDTH // A_BLOCKS
    per = n_tiles // A_BLOCKS
    for j in range(A_BLOCKS):
        uj = u2[:, j * ub:(j + 1) * ub]
        bu_r = _dot(uj, bbr_ref[j])
        bu_i = _dot(uj, bbi_ref[j])
        for q in range(per):
            hre[j * per + q] = bu_r[:, q * LANES:(q + 1) * LANES]
            him[j * per + q] = bu_i[:, q * LANES:(q + 1) * LANES]

    def slab(t):
        return pl.ds(t, bsz, stride=tc)

    def advance(t, prev):
        for cb in range(n_tiles):
            lanes = slice(cb * LANES, (cb + 1) * LANES)
            ar, ai = ar_ref[:, lanes], ai_ref[:, lanes]
            pr, pi = prev(cb)
            hr = ar * pr - ai * pi + hre[cb, slab(t), :]
            hi = ar * pi + ai * pr + him[cb, slab(t), :]
            hre[cb, slab(t), :] = hr
            him[cb, slab(t), :] = hi

    advance(0, lambda cb: (st_r[cb], st_i[cb]))

    def body(t, carry):
        advance(t, lambda cb: (hre[cb, slab(t - 1), :], him[cb, slab(t - 1), :]))
        return carry

    lax.fori_loop(1, n_steps, body, 0)
    for cb in range(n_tiles):
        st_r[cb] = hre[cb, slab(n_steps - 1), :]
        st_i[cb] = him[cb, slab(n_steps - 1), :]

    def block(ref, j):
        return jnp.concatenate([ref[j * per + q] for q in range(per)], axis=-1)

    y = jnp.concatenate(
        [_dot(block(hre, j), cr_ref[j]) - _dot(block(him, j), ci_ref[j]) for j in range(A_BLOCKS)],
        axis=-1) + d_ref[...] * u2
    y = 0.5 * y * (1.0 + jnp.tanh(math.sqrt(2.0 / math.pi) * (y + 0.044715 * (y * y * y))))
    y = y * _sigmoid(_dot(y, gw_ref[...]) + gb_ref[...])
    out = y * _silu(za_ref[...].reshape(bsz * tc, A_WIDTH))
    out_ref[...] = out.reshape(bsz, tc, A_WIDTH)

    @pl.when(i == pl.num_programs(0) - 1)
    def _():
        for cb in range(n_tiles):
            hfr_ref[:, cb * LANES:(cb + 1) * LANES] = st_r[cb]
            hfi_ref[:, cb * LANES:(cb + 1) * LANES] = st_i[cb]


def _s5_mixer(proj, h0r, h0i, sp, *, tc, t_valid):
    bsz, t, _ = proj.shape
    n_steps = tc if t_valid is None else t_valid
    full = lambda a: pl.BlockSpec(a.shape, lambda i: (0,) * a.ndim)
    col = lambda c: pl.BlockSpec((bsz, tc, A_WIDTH), lambda i: (0, i, c))
    weights = [sp["ab_re"], sp["ab_im"], sp["bb_re"], sp["bb_im"], sp["c_re"], sp["c_im"],
               sp["d"], sp["glu_w"], sp["glu_b"]]
    state = jax.ShapeDtypeStruct((bsz, A_LANES), F32)
    return pl.pallas_call(
        functools.partial(_s5_kernel, bsz=bsz, tc=tc, n_steps=n_steps), grid=(t // tc,),
        in_specs=[col(0), col(1), full(h0r), full(h0i)] + [full(a) for a in weights],
        out_specs=[pl.BlockSpec((bsz, tc, A_WIDTH), lambda i: (0, i, 0)), full(h0r), full(h0i)],
        out_shape=[jax.ShapeDtypeStruct((bsz, t, A_WIDTH), F32), state, state],
        scratch_shapes=[pltpu.VMEM((A_LANES // LANES, bsz * tc, LANES), F32)] * 2
        + [pltpu.VMEM((A_LANES // LANES, bsz, LANES), F32)] * 2,
        compiler_params=_params(("arbitrary",)), name="s5_mixer",
    )(proj, proj, h0r, h0i, *weights)


def _hgrn_kernel(q_ref, f_ref, i_ref, zb_ref, lb_ref, nw_ref, s0_ref, out_ref, sfin_ref, s_scr,
                 *, bb, chunk, n_chunks, t_valid, use_lb):
    c = chunk
    tb = pl.program_id(1)

    @pl.when(tb == 0)
    def _():
        s_scr[...] = s0_ref[...].reshape(bb * B_HEADS, B_DIM, B_DIM)

    incl, _, _ = _tri_masks(c)
    tri_f = incl.astype(F32)

    def body(ci, carry):
        off = pl.multiple_of(ci * c, c)
        sl = pl.ds(off, c)
        q = _silu(q_ref[:, sl, :])
        f = f_ref[:, sl, :]
        if use_lb:
            lb = lb_ref[...]
            log_f = jnp.log(lb + (1.0 - lb) * _sigmoid(f))
            k = (1.0 - lb) * _sigmoid(-f)
        else:
            log_f = jnp.minimum(f, 0.0) - jnp.log1p(jnp.exp(-jnp.abs(f)))
            k = _sigmoid(-f)
        if t_valid is not None:
            tok = tb * (n_chunks * c) + off + lax.broadcasted_iota(jnp.int32, (1, c, 1), 1)
            log_f = jnp.where(tok < t_valid, log_f, 0.0)
            k = jnp.where(tok < t_valid, k, 0.0)
        cum = _cumsum_time(tri_f, log_f)
        last = cum[:, c - 1:c, :]
        qh = _split_heads(q * jnp.exp(cum), B_DIM)
        kh = _split_heads(k * jnp.exp(-cum), B_DIM)
        ke = _split_heads(k * jnp.exp(last - cum), B_DIM)
        vh = _split_heads(i_ref[:, sl, :], B_DIM)
        gl = _split_heads(jnp.exp(last), B_DIM)
        s = s_scr[...]
        att = jnp.where(incl[None], _nt(qh, kh), 0.0)
        o = _nn(att, vh) + _nn(qh, s)
        g_col = jnp.swapaxes(jnp.broadcast_to(gl, s.shape), 1, 2)
        s_scr[...] = g_col * s + _tn(ke, vh)
        o = o * lax.rsqrt(jnp.mean(o * o, axis=-1, keepdims=True) + RMS_EPS) * nw_ref[...]
        out_ref[:, sl, :] = _merge_heads(o, bb) * _silu(zb_ref[:, sl, :])
        return carry

    lax.fori_loop(0, n_chunks, body, 0)

    @pl.when(tb == pl.num_programs(1) - 1)
    def _():
        sfin_ref[...] = s_scr[...].reshape(bb, B_HEADS, B_DIM, B_DIM)


def _hgrn_mixer(proj, s0, lb, nw, *, bb, tblock, chunk, t_valid, use_lb):
    bsz, t, _ = proj.shape
    col = lambda c: pl.BlockSpec((bb, tblock, B_WIDTH), lambda i, j: (i, j, c))
    vec = lambda a: pl.BlockSpec(a.shape, lambda i, j: (0, 0))
    st = pl.BlockSpec((bb, B_HEADS, B_DIM, B_DIM), lambda i, j: (i, 0, 0, 0))
    return pl.pallas_call(
        functools.partial(_hgrn_kernel, bb=bb, chunk=chunk, n_chunks=tblock // chunk, t_valid=t_valid,
                          use_lb=use_lb),
        grid=(bsz // bb, t // tblock),
        in_specs=[col(2), col(3), col(4), col(5), vec(lb), vec(nw), st],
        out_specs=[pl.BlockSpec((bb, tblock, B_WIDTH), lambda i, j: (i, j, 0)), st],
        out_shape=[jax.ShapeDtypeStruct((bsz, t, B_WIDTH), F32), jax.ShapeDtypeStruct(s0.shape, F32)],
        scratch_shapes=[pltpu.VMEM((bb * B_HEADS, B_DIM, B_DIM), F32)],
        compiler_params=_params(("parallel", "arbitrary")), name="hgrn2_mixer",
    )(proj, proj, proj, proj, lb, nw, s0)


def _shift_kernel(h_ref, nw_ref, sh_ref, xn_ref, xx_ref, carry, *, tblock):
    @pl.when(pl.program_id(1) == 0)
    def _():
        carry[...] = sh_ref[...]

    xn = _rms(h_ref[...], nw_ref[...])
    row = lax.broadcasted_iota(jnp.int32, (1, tblock, 1), 1)
    x_prev = jnp.where(row == 0, carry[...], pltpu.roll(xn, 1, axis=1))
    xn_ref[...] = xn
    xx_ref[...] = x_prev - xn
    carry[...] = xn[:, tblock - 1:tblock, :]


def _norm_shift(h, nw, shift, *, bb, tblock):
    bsz, t, d = h.shape
    tok = pl.BlockSpec((bb, tblock, d), lambda i, j: (i, j, 0))
    out = jax.ShapeDtypeStruct((bsz, t, d), F32)
    return pl.pallas_call(
        functools.partial(_shift_kernel, tblock=tblock), grid=(bsz // bb, t // tblock),
        in_specs=[tok, pl.BlockSpec((1, d), lambda i, j: (0, 0)), pl.BlockSpec((bb, 1, d), lambda i, j: (i, 0, 0))],
        out_specs=[tok, tok], out_shape=[out, out],
        scratch_shapes=[pltpu.VMEM((bb, 1, d), F32)],
        compiler_params=_params(("parallel", "arbitrary")), name="norm_token_shift",
    )(h, nw, shift.reshape(bsz, 1, d))


def _mixproj_kernel(xn_ref, xx_ref, mix_ref, w_ref, o_ref):
    o_ref[0] = _dot(xn_ref[...] + xx_ref[...] * mix_ref[0], w_ref[0])


def _mix_proj(xn, xx, mix, w, *, tm, tn):
    m, k = xn.shape
    p, _, n = w.shape
    tok = pl.BlockSpec((tm, k), lambda i, q, j: (i, 0))
    return pl.pallas_call(
        _mixproj_kernel, grid=(m // tm, p, n // tn),
        in_specs=[tok, tok, pl.BlockSpec((1, 1, k), lambda i, q, j: (q, 0, 0)),
                  pl.BlockSpec((1, k, tn), lambda i, q, j: (q, 0, j))],
        out_specs=pl.BlockSpec((1, tm, tn), lambda i, q, j: (q, i, j)),
        out_shape=jax.ShapeDtypeStruct((p, m, n), F32),
        compiler_params=_params(("parallel", "arbitrary", "arbitrary")), name="rwkv_rkvz_proj",
    )(xn, xx, mix.reshape(p, 1, k), w)


def _lora_kernel(*refs, vres):
    xn_ref, xx_ref, mix_ref, w0, w1, w2, a0, a1, a2, g1, g2 = refs[:11]
    rest = refs[11:]
    xn, xx = xn_ref[...], xx_ref[...]
    mixed = lambda m: xn + xx * mix_ref[m:m + 1, :]
    if vres:
        v0, v1, v2, lw_ref, al_ref, g_ref, vm_ref = rest
        vm_ref[...] = _sigmoid(v0[...] + _dot(_dot(mixed(2), v1[...]), v2[...]))
    else:
        lw_ref, al_ref, g_ref = rest
    lw_ref[...] = -DECAY_SCALE * _sigmoid(w0[...] + _dot(jnp.tanh(_dot(mixed(0), w1[...])), w2[...]))
    al_ref[...] = _sigmoid(a0[...] + _dot(_dot(mixed(1), a1[...]), a2[...]))
    g_ref[...] = _dot(_sigmoid(_dot(mixed(3), g1[...])), g2[...])


def _lora_paths(xn, xx, mix, weights, *, tm, vres):
    m, d = xn.shape
    tok = pl.BlockSpec((tm, d), lambda i: (i, 0))
    full = lambda a: pl.BlockSpec(a.shape, lambda i: (0, 0))
    n_out = 4 if vres else 3
    return pl.pallas_call(
        functools.partial(_lora_kernel, vres=vres), grid=(m // tm,),
        in_specs=[tok, tok, full(mix)] + [full(a) for a in weights],
        out_specs=[tok] * n_out, out_shape=[jax.ShapeDtypeStruct((m, d), F32)] * n_out,
        compiler_params=_params(("parallel",)), name="rwkv_lora_paths",
    )(xn, xx, mix, *weights)


def _rwkv_kernel(*refs, bb, chunk, n_chunks, t_valid, vres):
    r_ref, k_ref, v_ref, z_ref, lw_ref, al_ref, g_ref = refs[:7]
    if vres:
        vm_ref, vf_ref = refs[7:9]
        refs = refs[9:]
    else:
        refs = refs[7:]
    kk_ref, ka_ref, rk_ref, lnw_ref, lnb_ref, s0_ref, y_ref, sfin_ref, s_scr = refs
    c = chunk
    nh = bb * C_HEADS
    tb = pl.program_id(1)

    @pl.when(tb == 0)
    def _():
        s_scr[...] = s0_ref[...].reshape(nh, C_HEAD, C_HEAD)

    incl, strict, diag = _tri_masks(c)
    tri_f = incl.astype(F32)
    eye = diag.astype(F32)[None]
    heads = lambda x: _split_heads(x, C_HEAD)

    def body(ci, carry):
        off = pl.multiple_of(ci * c, c)
        sl = pl.ds(off, c)
        r = r_ref[0, :, sl, :]
        k = k_ref[0, :, sl, :]
        v = v_ref[0, :, sl, :]
        lw = lw_ref[:, sl, :]
        al = al_ref[:, sl, :]
        if vres:
            v = v + (vf_ref[0, :, sl, :] - v) * vm_ref[:, sl, :]
        kk = heads(k * kk_ref[...])
        kk = kk / jnp.maximum(jnp.sqrt(jnp.sum(kk * kk, axis=-1, keepdims=True)), 1e-12)
        k = k * (1.0 + (al - 1.0) * ka_ref[...])
        bonus = jnp.sum(heads(r * k * rk_ref[...]), axis=-1, keepdims=True)
        if t_valid is not None:
            tok = tb * (n_chunks * c) + off + lax.broadcasted_iota(jnp.int32, (1, c, 1), 1)
            lw = jnp.where(tok < t_valid, lw, 0.0)
            k = jnp.where(tok < t_valid, k, 0.0)
            al = jnp.where(tok < t_valid, al, 0.0)
        cum = _cumsum_time(tri_f, lw)
        g_in = jnp.exp(cum)
        g_inv = jnp.exp(-cum)
        rt = heads(r * g_in)
        at = -kk * heads(jnp.exp(cum - lw))
        kt = heads(k * g_inv)
        bt = kk * heads(al * g_inv)
        vh = heads(v)
        g_last = heads(g_in[:, c - 1:c, :])
        s = s_scr[...]
        x = jnp.concatenate([at, rt], axis=1)
        a_k = _nt(x, kt)
        a_b = _nt(x, bt)
        x_s = _nt(x, s)
        a_ak = jnp.where(strict[None], a_k[:, :c], 0.0)
        a_rk = jnp.where(incl[None], a_k[:, c:], 0.0)
        n_ab = jnp.where(strict[None], a_b[:, :c], 0.0)
        a_rb = jnp.where(incl[None], a_b[:, c:], 0.0)
        w_m = x_s[:, :c] + _nn(a_ak, vh)
        t_m = eye + n_ab
        p_m = n_ab
        for _ in range(int(math.log2(c)) - 1):
            p_m = _nn(p_m, p_m)
            t_m = t_m + _nn(t_m, p_m)
        u = _nn(t_m, w_m)
        y = x_s[:, c:] + _nn(a_rk, vh) + _nn(a_rb, u)
        s_scr[...] = (s + _tn(vh, kt) + _tn(u, bt)) * g_last
        mu = jnp.mean(y, axis=-1, keepdims=True)
        dev = y - mu
        var = jnp.mean(dev * dev, axis=-1, keepdims=True)
        yn = _merge_heads(dev * lax.rsqrt(var + GN_EPS), bb) * lnw_ref[...] + lnb_ref[...]
        gate = g_ref[:, sl, :] * _silu(z_ref[0, :, sl, :])
        y_ref[:, sl, :] = (yn + _merge_heads(bonus * vh, bb)) * gate
        return carry

    lax.fori_loop(0, n_chunks, body, 0)

    @pl.when(tb == pl.num_programs(1) - 1)
    def _():
        sfin_ref[...] = s_scr[...].reshape(bb, C_HEADS, C_HEAD, C_HEAD)


def _rwkv_mixer(rkvz, lw, al, g, vmix, v_first, vecs, s0, *, bb, tblock, chunk, t_valid):
    _, bsz, t, d = rkvz.shape
    vres = vmix is not None
    tok = pl.BlockSpec((bb, tblock, d), lambda i, j: (i, j, 0))
    stk = lambda p: pl.BlockSpec((1, bb, tblock, d), lambda i, j: (p, i, j, 0))
    vec = pl.BlockSpec((1, d), lambda i, j: (0, 0))
    st = pl.BlockSpec((bb, C_HEADS, C_HEAD, C_HEAD), lambda i, j: (i, 0, 0, 0))
    in_specs = [stk(0), stk(1), stk(2), stk(3), tok, tok, tok]
    args = [rkvz, rkvz, rkvz, rkvz, lw, al, g]
    if vres:
        in_specs += [tok, stk(2)]
        args += [vmix, v_first]
    in_specs += [vec] * 5 + [st]
    args += [*vecs, s0]
    return pl.pallas_call(
        functools.partial(_rwkv_kernel, bb=bb, chunk=chunk, n_chunks=tblock // chunk, t_valid=t_valid, vres=vres),
        grid=(bsz // bb, t // tblock),
        in_specs=in_specs, out_specs=[tok, st],
        out_shape=[jax.ShapeDtypeStruct((bsz, t, d), F32), jax.ShapeDtypeStruct(s0.shape, F32)],
        scratch_shapes=[pltpu.VMEM((bb * C_HEADS, C_HEAD, C_HEAD), F32)],
        compiler_params=_params(("parallel", "arbitrary")), name="rwkv7_mixer",
    )(*args)


def _s5_params(w, j):
    lr = jnp.minimum(w["ssm_lambda_re"][j], -1e-4)
    li = w["ssm_lambda_im"][j]
    step = jnp.exp(w["ssm_log_step"][j])[:, None]
    mag = jnp.exp(lr * step)
    ab_re = mag * jnp.cos(li * step)
    ab_im = mag * jnp.sin(li * step)
    den = lr * lr + li * li
    nr = ab_re - 1.0
    cr = (nr * lr + ab_im * li) / den
    ci = (ab_im * lr - nr * li) / den
    b_re, b_im = w["ssm_b_re"][j], w["ssm_b_im"][j]
    bb_re = cr[..., None] * b_re - ci[..., None] * b_im
    bb_im = cr[..., None] * b_im + ci[..., None] * b_re
    gpb = A_GROUPS // A_BLOCKS
    eye = jnp.eye(gpb, dtype=F32)

    def pack_in(bb):
        bb = bb.reshape(A_BLOCKS, gpb, A_STATE, A_GROUP)
        return jnp.einsum("bgph,gk->bghkp", bb, eye).reshape(A_BLOCKS, gpb * A_GROUP, gpb * A_STATE)

    def pack_out(c):
        c = c.reshape(A_BLOCKS, gpb, A_GROUP, A_STATE)
        return jnp.einsum("bghp,gk->bgpkh", c, eye).reshape(A_BLOCKS, gpb * A_STATE, gpb * A_GROUP)

    return dict(ab_re=ab_re.reshape(1, A_LANES), ab_im=ab_im.reshape(1, A_LANES),
                bb_re=pack_in(bb_re), bb_im=pack_in(bb_im),
                c_re=pack_out(w["ssm_c_re"][j]), c_im=pack_out(w["ssm_c_im"][j]),
                d=w["ssm_d"][j].reshape(1, A_WIDTH), glu_w=w["ssm_glu_w"][j],
                glu_b=w["ssm_glu_b"][j].reshape(1, A_WIDTH))


def _trunk(x, ssm_re, ssm_im, hgrn, wkv, shift, w, cfg):
    bsz, t, d = x.shape
    t_valid = cfg["t_valid"]
    m = bsz * t
    h = x.reshape(m, d)
    n_re, n_im, n_hg, n_wkv, n_sh = [], [], [], [], []
    lbs = jax.nn.softmax(w["hgrn_lower_bounds"], axis=0)
    lbs = jnp.cumsum(lbs, axis=0) - lbs[0]
    v_first = None
    y = None
    depth = w["norm_w"].shape[0]
    for layer in range(depth):
        j = layer // 2
        nw = w["norm_w"][layer].reshape(1, d)
        last = layer == depth - 1
        final_nw = w["final_norm_w"].reshape(1, d) if last else None
        if layer % 2 == 0:
            proj = _norm_matmul(h, nw, w["even_w_in"][j], tm=cfg["tm"], tn=512).reshape(bsz, t, -1)
            out_a, hr, hi = _s5_mixer(proj, ssm_re[j].reshape(bsz, A_LANES), ssm_im[j].reshape(bsz, A_LANES),
                                      _s5_params(w, j), tc=cfg["s5_tc"], t_valid=t_valid)
            out_b, sb = _hgrn_mixer(proj, hgrn[j], lbs[j].reshape(1, B_WIDTH), w["hgrn_norm_w"][j].reshape(1, B_DIM),
                                    bb=cfg["bb"], tblock=cfg["tblock"], chunk=cfg["chunk"], t_valid=t_valid,
                                    use_lb=j > 0)
            res = _proj_residual(h, [out_a.reshape(m, A_WIDTH), out_b.reshape(m, B_WIDTH)], w["even_w_out"][j],
                                 final_nw, tm=cfg["tm"])
            n_re.append(hr.reshape(bsz, A_GROUPS, A_STATE))
            n_im.append(hi.reshape(bsz, A_GROUPS, A_STATE))
            n_hg.append(sb)
        else:
            xn, xx = _norm_shift(h.reshape(bsz, t, d), nw, shift[j], bb=cfg["bb"], tblock=cfg["tblock"])
            n_sh.append(xn[:, (t if t_valid is None else t_valid) - 1])
            xn2, xx2 = xn.reshape(m, d), xx.reshape(m, d)
            mix = w["rw_mix"][j]
            rkvz = _mix_proj(xn2, xx2, mix[jnp.array([0, 2, 3, 5])], w["rw_w_rkvz"][j], tm=cfg["tm"], tn=512)
            rkvz = rkvz.reshape(4, bsz, t, d)
            vres = v_first is not None
            row = lambda a: a.reshape(1, -1)
            lora_w = [row(w["rw_w0"][j]), w["rw_w1"][j], w["rw_w2"][j], row(w["rw_a0"][j]), w["rw_a1"][j],
                      w["rw_a2"][j], w["rw_g1"][j], w["rw_g2"][j]]
            if vres:
                lora_w += [row(w["rw_v0"][j - 1]), w["rw_v1"][j - 1], w["rw_v2"][j - 1]]
            outs = _lora_paths(xn2, xx2, mix[jnp.array([1, 4, 3, 5])], lora_w, tm=cfg["tm"], vres=vres)
            lw, al, g = (a.reshape(bsz, t, d) for a in outs[:3])
            vmix = outs[3].reshape(bsz, t, d) if vres else None
            vecs = [row(w[n][j]) for n in ("rw_k_k", "rw_k_a", "rw_r_k", "rw_ln_w", "rw_ln_b")]
            y_mix, sw = _rwkv_mixer(rkvz, lw, al, g, vmix, v_first, vecs, wkv[j], bb=cfg["bb"],
                                    tblock=cfg["tblock"], chunk=cfg["chunk"], t_valid=t_valid)
            if v_first is None:
                v_first = rkvz
            res = _proj_residual(h, [y_mix.reshape(m, d)], w["rw_w_o"][j], final_nw, tm=cfg["tm"])
            n_wkv.append(sw)
        if last:
            h, y = res
        else:
            h = res[0]
    return (y.reshape(bsz, t, d), jnp.stack(n_re), jnp.stack(n_im), jnp.stack(n_hg), jnp.stack(n_wkv),
            jnp.stack(n_sh))


PROMPT_CFG = dict(t_valid=None, tm=512, s5_tc=64, bb=1, tblock=256, chunk=16)
SAMPLE_PAD = 8
SAMPLE_CFG = dict(t_valid=4, tm=512, s5_tc=SAMPLE_PAD, bb=8, tblock=SAMPLE_PAD, chunk=SAMPLE_PAD)


def kernel(x_prompt, x_sample, state_ssm_re, state_ssm_im, state_hgrn, state_wkv, state_shift, norm_w, final_norm_w, even_w_in, even_w_out, ssm_lambda_re, ssm_lambda_im, ssm_log_step, ssm_b_re, ssm_b_im, ssm_c_re, ssm_c_im, ssm_d, ssm_glu_w, ssm_glu_b, hgrn_lower_bounds, hgrn_norm_w, rw_mix, rw_w_rkvz, rw_w0, rw_w1, rw_w2, rw_a0, rw_a1, rw_a2, rw_v0, rw_v1, rw_v2, rw_g1, rw_g2, rw_k_k, rw_k_a, rw_r_k, rw_ln_w, rw_ln_b, rw_w_o):
    w = dict(norm_w=norm_w, final_norm_w=final_norm_w, even_w_in=even_w_in, even_w_out=even_w_out,
             ssm_lambda_re=ssm_lambda_re, ssm_lambda_im=ssm_lambda_im, ssm_log_step=ssm_log_step,
             ssm_b_re=ssm_b_re, ssm_b_im=ssm_b_im, ssm_c_re=ssm_c_re, ssm_c_im=ssm_c_im, ssm_d=ssm_d,
             ssm_glu_w=ssm_glu_w, ssm_glu_b=ssm_glu_b, hgrn_lower_bounds=hgrn_lower_bounds,
             hgrn_norm_w=hgrn_norm_w, rw_mix=rw_mix, rw_w_rkvz=rw_w_rkvz, rw_w0=rw_w0, rw_w1=rw_w1,
             rw_w2=rw_w2, rw_a0=rw_a0, rw_a1=rw_a1, rw_a2=rw_a2, rw_v0=rw_v0, rw_v1=rw_v1, rw_v2=rw_v2,
             rw_g1=rw_g1, rw_g2=rw_g2, rw_k_k=rw_k_k, rw_k_a=rw_k_a, rw_r_k=rw_r_k,
             rw_ln_w=rw_ln_w, rw_ln_b=rw_ln_b, rw_w_o=rw_w_o)
    n_even, n_odd = state_hgrn.shape[0], state_wkv.shape[0]
    bp, bs, ts = x_prompt.shape[0], x_sample.shape[0], x_sample.shape[1]
    assert ts == SAMPLE_CFG["t_valid"]
    z_re = jnp.zeros((n_even, bp, A_GROUPS, A_STATE), F32)
    z_hg = jnp.zeros((n_even, bp, B_HEADS, B_DIM, B_DIM), F32)
    z_wkv = jnp.zeros((n_odd, bp, C_HEADS, C_HEAD, C_HEAD), F32)
    z_sh = jnp.zeros((n_odd, bp, D_MODEL), F32)
    prompt = _trunk(x_prompt, z_re, z_re, z_hg, z_wkv, z_sh, w, PROMPT_CFG)
    xs = jnp.pad(x_sample, ((0, 0), (0, SAMPLE_PAD - ts), (0, 0)))
    sample = _trunk(xs, state_ssm_re, state_ssm_im, state_hgrn, state_wkv, state_shift, w, SAMPLE_CFG)
    y_sample = sample[0][:, :ts]
    return (prompt[0], y_sample, *prompt[1:], *sample[1:])
```

```python
import functools
import math

import jax
import jax.numpy as jnp
from jax import lax
from jax.experimental import pallas as pl
from jax.experimental.pallas import tpu as pltpu

F32 = jnp.float32
BF16 = jnp.bfloat16

D_MODEL = 1024
A_WIDTH = 512
A_GROUP = 16
A_GROUPS = 32
A_STATE = 64
A_LANES = A_GROUPS * A_STATE
A_BLOCKS = 4
B_WIDTH = 512
B_HEADS = 4
B_DIM = 128
C_HEADS = 16
C_HEAD = 64
DECAY_SCALE = math.exp(-0.5)
RMS_EPS = 1e-6
GN_EPS = 64e-5
VMEM_LIMIT = 56 * 1024 * 1024
LANES = 128


def _dot(a, b):
    return jnp.dot(a.astype(b.dtype), b, preferred_element_type=F32)


def _bdot(a, b, contract):
    return lax.dot_general(a, b, (contract, ((0,), (0,))), preferred_element_type=F32)


def _nn(a, b):
    return _bdot(a, b, ((2,), (1,)))


def _nt(a, b):
    return _bdot(a, b, ((2,), (2,)))


def _tn(a, b):
    return _bdot(a, b, ((1,), (1,)))


def _rms(x, w):
    return x * lax.rsqrt(jnp.mean(x * x, axis=-1, keepdims=True) + RMS_EPS) * w


def _sigmoid(x):
    return 1.0 / (1.0 + jnp.exp(-x))


def _silu(x):
    return x * _sigmoid(x)


def _params(sem):
    return pltpu.CompilerParams(dimension_semantics=sem, vmem_limit_bytes=VMEM_LIMIT)


def _tri_masks(c):
    row = lax.broadcasted_iota(jnp.int32, (c, c), 0)
    col = lax.broadcasted_iota(jnp.int32, (c, c), 1)
    return row >= col, row > col, row == col


def _split_heads(x, width):
    bb, _, d = x.shape
    return jnp.stack([x[i, :, h * width:(h + 1) * width] for i in range(bb) for h in range(d // width)], axis=0)


def _merge_heads(x, bb):
    h = x.shape[0] // bb
    return jnp.stack([jnp.concatenate([x[i * h + j] for j in range(h)], axis=-1) for i in range(bb)], axis=0)


def _cumsum_time(tri_f, x):
    hi = x.astype(BF16).astype(F32)
    rest = x - hi
    mid = rest.astype(BF16).astype(F32)
    lo = rest - mid
    return jnp.stack([sum(jnp.dot(tri_f, p[i], preferred_element_type=F32) for p in (hi, mid, lo))
                      for i in range(x.shape[0])], axis=0)


def _norm_matmul_kernel(x_ref, nw_ref, w_ref, o_ref):
    o_ref[...] = _dot(_rms(x_ref[...], nw_ref[...]), w_ref[...])


def _norm_matmul(x, nw, w, *, tm, tn):
    m, k = x.shape
    n = w.shape[1]
    return pl.pallas_call(
        _norm_matmul_kernel, grid=(m // tm, n // tn),
        in_specs=[pl.BlockSpec((tm, k), lambda i, j: (i, 0)),
                  pl.BlockSpec((1, k), lambda i, j: (0, 0)),
                  pl.BlockSpec((k, tn), lambda i, j: (0, j))],
        out_specs=pl.BlockSpec((tm, tn), lambda i, j: (i, j)),
        out_shape=jax.ShapeDtypeStruct((m, n), F32),
        compiler_params=_params(("parallel", "arbitrary")), name="norm_in_proj",
    )(x, nw, w)


def _proj_residual_kernel(*refs, n_x, final_norm):
    h_ref, x_refs, w_ref = refs[0], refs[1:1 + n_x], refs[1 + n_x]
    x = jnp.concatenate([r[...] for r in x_refs], axis=-1) if n_x > 1 else x_refs[0][...]
    h = h_ref[...] + _dot(x, w_ref[...])
    if final_norm:
        nw_ref, h_out, y_out = refs[2 + n_x:]
        y_out[...] = _rms(h, nw_ref[...])
    else:
        h_out = refs[2 + n_x]
    h_out[...] = h


def _proj_residual(h, xs, w, final_nw=None, *, tm):
    m, n = h.shape
    final_norm = final_nw is not None
    row = lambda width: pl.BlockSpec((tm, width), lambda i: (i, 0))
    in_specs = [row(n)] + [row(x.shape[1]) for x in xs] + [pl.BlockSpec(w.shape, lambda i: (0, 0))]
    args = [h, *xs, w]
    out_specs, out_shape = [row(n)], [jax.ShapeDtypeStruct((m, n), F32)]
    if final_norm:
        in_specs.append(pl.BlockSpec((1, n), lambda i: (0, 0)))
        args.append(final_nw)
        out_specs.append(row(n))
        out_shape.append(jax.ShapeDtypeStruct((m, n), F32))
    return pl.pallas_call(
        functools.partial(_proj_residual_kernel, n_x=len(xs), final_norm=final_norm), grid=(m // tm,),
        in_specs=in_specs, out_specs=out_specs, out_shape=out_shape,
        compiler_params=_params(("parallel",)), name="out_proj_residual",
    )(*args)


def _s5_kernel(u_ref, za_ref, h0r_ref, h0i_ref, ar_ref, ai_ref, bbr_ref, bbi_ref, cr_ref, ci_ref,
               d_ref, gw_ref, gb_ref, out_ref, hfr_ref, hfi_ref, hre, him, bur, bui, st_r, st_i, wbm, wtm,
               *, bsz, tc, n_steps):
    i = pl.program_id(1)
    rows = bsz * tc
    n_tiles = A_LANES // LANES
    u_tiles = A_WIDTH // LANES
    per = n_tiles // A_BLOCKS
    assert A_WIDTH // A_BLOCKS == LANES

    @pl.when(i == 0)
    def _():
        for cb in range(n_tiles):
            st_r[cb] = h0r_ref[:, cb * LANES:(cb + 1) * LANES]
            st_i[cb] = h0i_ref[:, cb * LANES:(cb + 1) * LANES]

    def bm(t):
        return pl.ds(t, bsz, stride=tc)

    def tm(t):
        return pl.ds(pl.multiple_of(t * bsz, bsz), bsz)

    def reorder(src, dst, src_rows, dst_rows):
        def body(t, carry):
            for q in range(u_tiles):
                dst.at[q][dst_rows(t), :] = src.at[q][src_rows(t), :]
            return carry
        lax.fori_loop(0, tc, body, 0)

    u_bm = u_ref[...].reshape(rows, A_WIDTH)
    for q in range(u_tiles):
        wbm[q] = u_bm[:, q * LANES:(q + 1) * LANES]
    reorder(wbm, wtm, bm, tm)

    for j in range(A_BLOCKS):
        bu_r = _dot(wtm[j], bbr_ref[j])
        bu_i = _dot(wtm[j], bbi_ref[j])
        for q in range(per):
            bur[j * per + q] = bu_r[:, q * LANES:(q + 1) * LANES]
            bui[j * per + q] = bu_i[:, q * LANES:(q + 1) * LANES]

    def advance(t, prev):
        new = []
        for cb in range(n_tiles):
            lanes = slice(cb * LANES, (cb + 1) * LANES)
            ar, ai = ar_ref[:, lanes], ai_ref[:, lanes]
            pr, pi = prev[2 * cb], prev[2 * cb + 1]
            hr = ar * pr - ai * pi + bur[cb, tm(t), :]
            hi = ar * pi + ai * pr + bui[cb, tm(t), :]
            hre[cb, tm(t), :] = hr
            him[cb, tm(t), :] = hi
            new += [hr, hi]
        return tuple(new)

    state = tuple(ref[cb] for cb in range(n_tiles) for ref in (st_r, st_i))
    if n_steps <= 8:
        for t in range(n_steps):
            state = advance(t, state)
    else:
        state = lax.fori_loop(0, n_steps, advance, state)
    for cb in range(n_tiles):
        st_r[cb] = state[2 * cb]
        st_i[cb] = state[2 * cb + 1]
    if n_steps < tc:
        for cb in range(n_tiles):
            hre[cb, pl.ds(n_steps * bsz, (tc - n_steps) * bsz), :] = bur[cb, pl.ds(n_steps * bsz, (tc - n_steps) * bsz), :]
            him[cb, pl.ds(n_steps * bsz, (tc - n_steps) * bsz), :] = bui[cb, pl.ds(n_steps * bsz, (tc - n_steps) * bsz), :]

    def block(ref, j):
        return jnp.concatenate([ref[j * per + q] for q in range(per)], axis=-1)

    u_tm = jnp.concatenate([wtm[q] for q in range(u_tiles)], axis=-1)
    y = jnp.concatenate(
        [_dot(block(hre, j), cr_ref[j]) - _dot(block(him, j), ci_ref[j]) for j in range(A_BLOCKS)],
        axis=-1) + d_ref[...] * u_tm
    y = 0.5 * y * (1.0 + jnp.tanh(math.sqrt(2.0 / math.pi) * (y + 0.044715 * (y * y * y))))
    y = y * _sigmoid(_dot(y, gw_ref[...]) + gb_ref[...])
    for q in range(u_tiles):
        wtm[q] = y[:, q * LANES:(q + 1) * LANES]
    reorder(wtm, wbm, tm, bm)
    y_bm = jnp.concatenate([wbm[q] for q in range(u_tiles)], axis=-1)
    out = y_bm * _silu(za_ref[...].reshape(rows, A_WIDTH))
    out_ref[...] = out.reshape(bsz, tc, A_WIDTH)

    @pl.when(i == pl.num_programs(1) - 1)
    def _():
        for cb in range(n_tiles):
            hfr_ref[:, cb * LANES:(cb + 1) * LANES] = st_r[cb]
            hfi_ref[:, cb * LANES:(cb + 1) * LANES] = st_i[cb]


def _s5_mixer(proj, h0r, h0i, sp, *, bb, tc, t_valid):
    bsz, t, _ = proj.shape
    n_steps = tc if t_valid is None else t_valid
    full = lambda a: pl.BlockSpec(a.shape, lambda b, i: (0,) * a.ndim)
    col = lambda c: pl.BlockSpec((bb, tc, A_WIDTH), lambda b, i: (b, i, c))
    st = pl.BlockSpec((bb, A_LANES), lambda b, i: (b, 0))
    weights = [sp["ab_re"], sp["ab_im"], sp["bb_re"], sp["bb_im"], sp["c_re"], sp["c_im"],
               sp["d"], sp["glu_w"], sp["glu_b"]]
    state = jax.ShapeDtypeStruct((bsz, A_LANES), F32)
    return pl.pallas_call(
        functools.partial(_s5_kernel, bsz=bb, tc=tc, n_steps=n_steps), grid=(bsz // bb, t // tc),
        in_specs=[col(0), col(1), st, st] + [full(a) for a in weights],
        out_specs=[col(0), st, st],
        out_shape=[jax.ShapeDtypeStruct((bsz, t, A_WIDTH), F32), state, state],
        scratch_shapes=[pltpu.VMEM((A_LANES // LANES, bb * tc, LANES), F32)] * 4
        + [pltpu.VMEM((A_LANES // LANES, bb, LANES), F32)] * 2
        + [pltpu.VMEM((A_WIDTH // LANES, bb * tc, LANES), F32)] * 2,
        compiler_params=_params(("parallel", "arbitrary")), name="s5_mixer",
    )(proj, proj, h0r, h0i, *weights)


def _hgrn_kernel(q_ref, f_ref, i_ref, zb_ref, lb_ref, nw_ref, s0_ref, out_ref, sfin_ref, s_scr,
                 *, bb, chunk, n_chunks, t_valid, use_lb):
    c = chunk
    tb = pl.program_id(1)

    @pl.when(tb == 0)
    def _():
        s_scr[...] = jnp.swapaxes(s0_ref[...].reshape(bb * B_HEADS, B_DIM, B_DIM), 1, 2)

    incl, _, _ = _tri_masks(c)
    tri_f = incl.astype(F32)

    def body(ci, carry):
        off = pl.multiple_of(ci * c, c)
        sl = pl.ds(off, c)
        q = _silu(q_ref[:, sl, :])
        f = f_ref[:, sl, :]
        if use_lb:
            lb = lb_ref[...]
            log_f = jnp.log(lb + (1.0 - lb) * _sigmoid(f))
            k = (1.0 - lb) * _sigmoid(-f)
        else:
            log_f = jnp.minimum(f, 0.0) - jnp.log1p(jnp.exp(-jnp.abs(f)))
            k = _sigmoid(-f)
        if t_valid is not None:
            tok = tb * (n_chunks * c) + off + lax.broadcasted_iota(jnp.int32, (1, c, 1), 1)
            log_f = jnp.where(tok < t_valid, log_f, 0.0)
            k = jnp.where(tok < t_valid, k, 0.0)
        cum = _cumsum_time(tri_f, log_f)
        last = cum[:, c - 1:c, :]
        qh = _split_heads(q * jnp.exp(cum), B_DIM)
        kh = _split_heads(k * jnp.exp(-cum), B_DIM)
        ke = _split_heads(k * jnp.exp(last - cum), B_DIM)
        vh = _split_heads(i_ref[:, sl, :], B_DIM)
        gl = _split_heads(jnp.exp(last), B_DIM)
        s = s_scr[...]
        att = jnp.where(incl[None], _nt(qh, kh), 0.0)
        o = _nn(att, vh) + _nt(qh, s)
        s_scr[...] = s * gl + _tn(vh, ke)
        o = o * lax.rsqrt(jnp.mean(o * o, axis=-1, keepdims=True) + RMS_EPS) * nw_ref[...]
        out_ref[:, sl, :] = _merge_heads(o, bb) * _silu(zb_ref[:, sl, :])
        return carry

    lax.fori_loop(0, n_chunks, body, 0)

    @pl.when(tb == pl.num_programs(1) - 1)
    def _():
        sfin_ref[...] = jnp.swapaxes(s_scr[...], 1, 2).reshape(bb, B_HEADS, B_DIM, B_DIM)


def _hgrn_mixer(proj, s0, lb, nw, *, bb, tblock, chunk, t_valid, use_lb):
    bsz, t, _ = proj.shape
    col = lambda c: pl.BlockSpec((bb, tblock, B_WIDTH), lambda i, j: (i, j, c))
    vec = lambda a: pl.BlockSpec(a.shape, lambda i, j: (0, 0))
    st = pl.BlockSpec((bb, B_HEADS, B_DIM, B_DIM), lambda i, j: (i, 0, 0, 0))
    return pl.pallas_call(
        functools.partial(_hgrn_kernel, bb=bb, chunk=chunk, n_chunks=tblock // chunk, t_valid=t_valid,
                          use_lb=use_lb),
        grid=(bsz // bb, t // tblock),
        in_specs=[col(2), col(3), col(4), col(5), vec(lb), vec(nw), st],
        out_specs=[pl.BlockSpec((bb, tblock, B_WIDTH), lambda i, j: (i, j, 0)), st],
        out_shape=[jax.ShapeDtypeStruct((bsz, t, B_WIDTH), F32), jax.ShapeDtypeStruct(s0.shape, F32)],
        scratch_shapes=[pltpu.VMEM((bb * B_HEADS, B_DIM, B_DIM), F32)],
        compiler_params=_params(("parallel", "arbitrary")), name="hgrn2_mixer",
    )(proj, proj, proj, proj, lb, nw, s0)


def _shift_kernel(h_ref, nw_ref, sh_ref, xn_ref, xx_ref, carry, *, tblock):
    @pl.when(pl.program_id(1) == 0)
    def _():
        carry[...] = sh_ref[...]

    xn = _rms(h_ref[...], nw_ref[...])
    row = lax.broadcasted_iota(jnp.int32, (1, tblock, 1), 1)
    x_prev = jnp.where(row == 0, carry[...], pltpu.roll(xn, 1, axis=1))
    xn_ref[...] = xn
    xx_ref[...] = x_prev - xn
    carry[...] = xn[:, tblock - 1:tblock, :]


def _norm_shift(h, nw, shift, *, bb, tblock):
    bsz, t, d = h.shape
    tok = pl.BlockSpec((bb, tblock, d), lambda i, j: (i, j, 0))
    out = jax.ShapeDtypeStruct((bsz, t, d), F32)
    return pl.pallas_call(
        functools.partial(_shift_kernel, tblock=tblock), grid=(bsz // bb, t // tblock),
        in_specs=[tok, pl.BlockSpec((1, d), lambda i, j: (0, 0)), pl.BlockSpec((bb, 1, d), lambda i, j: (i, 0, 0))],
        out_specs=[tok, tok], out_shape=[out, out],
        scratch_shapes=[pltpu.VMEM((bb, 1, d), F32)],
        compiler_params=_params(("parallel", "arbitrary")), name="norm_token_shift",
    )(h, nw, shift.reshape(bsz, 1, d))


def _mixproj_kernel(xn_ref, xx_ref, mix_ref, w_ref, o_ref):
    o_ref[0] = _dot(xn_ref[...] + xx_ref[...] * mix_ref[0], w_ref[0])


def _mix_proj(xn, xx, mix, w, *, tm, tn):
    m, k = xn.shape
    p, _, n = w.shape
    tok = pl.BlockSpec((tm, k), lambda i, q, j: (i, 0))
    return pl.pallas_call(
        _mixproj_kernel, grid=(m // tm, p, n // tn),
        in_specs=[tok, tok, pl.BlockSpec((1, 1, k), lambda i, q, j: (q, 0, 0)),
                  pl.BlockSpec((1, k, tn), lambda i, q, j: (q, 0, j))],
        out_specs=pl.BlockSpec((1, tm, tn), lambda i, q, j: (q, i, j)),
        out_shape=jax.ShapeDtypeStruct((p, m, n), F32),
        compiler_params=_params(("parallel", "arbitrary", "arbitrary")), name="rwkv_rkvz_proj",
    )(xn, xx, mix.reshape(p, 1, k), w)


def _lora_kernel(*refs, vres):
    xn_ref, xx_ref, mix_ref, w0, w1, w2, a0, a1, a2, g1, g2 = refs[:11]
    rest = refs[11:]
    xn, xx = xn_ref[...], xx_ref[...]
    mixed = lambda m: xn + xx * mix_ref[m:m + 1, :]
    if vres:
        v0, v1, v2, lw_ref, al_ref, g_ref, vm_ref = rest
        vm_ref[...] = _sigmoid(v0[...] + _dot(_dot(mixed(2), v1[...]), v2[...]))
    else:
        lw_ref, al_ref, g_ref = rest
    lw_ref[...] = -DECAY_SCALE * _sigmoid(w0[...] + _dot(jnp.tanh(_dot(mixed(0), w1[...])), w2[...]))
    al_ref[...] = _sigmoid(a0[...] + _dot(_dot(mixed(1), a1[...]), a2[...]))
    g_ref[...] = _dot(_sigmoid(_dot(mixed(3), g1[...])), g2[...])


def _lora_paths(xn, xx, mix, weights, *, tm, vres):
    m, d = xn.shape
    tok = pl.BlockSpec((tm, d), lambda i: (i, 0))
    full = lambda a: pl.BlockSpec(a.shape, lambda i: (0, 0))
    n_out = 4 if vres else 3
    return pl.pallas_call(
        functools.partial(_lora_kernel, vres=vres), grid=(m // tm,),
        in_specs=[tok, tok, full(mix)] + [full(a) for a in weights],
        out_specs=[tok] * n_out, out_shape=[jax.ShapeDtypeStruct((m, d), F32)] * n_out,
        compiler_params=_params(("parallel",)), name="rwkv_lora_paths",
    )(xn, xx, mix, *weights)


def _rwkv_kernel(*refs, bb, chunk, n_chunks, t_valid, vres):
    r_ref, k_ref, v_ref, z_ref, lw_ref, al_ref, g_ref = refs[:7]
    if vres:
        vm_ref, vf_ref = refs[7:9]
        refs = refs[9:]
    else:
        refs = refs[7:]
    kk_ref, ka_ref, rk_ref, lnw_ref, lnb_ref, s0_ref, y_ref, sfin_ref, s_scr = refs
    c = chunk
    nh = bb * C_HEADS
    tb = pl.program_id(1)

    @pl.when(tb == 0)
    def _():
        s_scr[...] = s0_ref[...].reshape(nh, C_HEAD, C_HEAD)

    incl, strict, diag = _tri_masks(c)
    tri_f = incl.astype(F32)
    eye = diag.astype(F32)[None]
    heads = lambda x: _split_heads(x, C_HEAD)

    def body(ci, carry):
        off = pl.multiple_of(ci * c, c)
        sl = pl.ds(off, c)
        r = r_ref[0, :, sl, :]
        k = k_ref[0, :, sl, :]
        v = v_ref[0, :, sl, :]
        lw = lw_ref[:, sl, :]
        al = al_ref[:, sl, :]
        if vres:
            v = v + (vf_ref[0, :, sl, :] - v) * vm_ref[:, sl, :]
        kk = heads(k * kk_ref[...])
        kk = kk / jnp.maximum(jnp.sqrt(jnp.sum(kk * kk, axis=-1, keepdims=True)), 1e-12)
        k = k * (1.0 + (al - 1.0) * ka_ref[...])
        bonus = jnp.sum(heads(r * k * rk_ref[...]), axis=-1, keepdims=True)
        if t_valid is not None:
            tok = tb * (n_chunks * c) + off + lax.broadcasted_iota(jnp.int32, (1, c, 1), 1)
            lw = jnp.where(tok < t_valid, lw, 0.0)
            k = jnp.where(tok < t_valid, k, 0.0)
            al = jnp.where(tok < t_valid, al, 0.0)
        cum = _cumsum_time(tri_f, lw)
        g_in = jnp.exp(cum)
        g_inv = jnp.exp(-cum)
        rt = heads(r * g_in)
        at = -kk * heads(jnp.exp(cum - lw))
        kt = heads(k * g_inv)
        bt = kk * heads(al * g_inv)
        vh = heads(v)
        g_last = heads(g_in[:, c - 1:c, :])
        s = s_scr[...]
        x = jnp.concatenate([at, rt], axis=1)
        a_k = _nt(x, kt)
        a_b = _nt(x, bt)
        x_s = _nt(x, s)
        a_ak = jnp.where(strict[None], a_k[:, :c], 0.0)
        a_rk = jnp.where(incl[None], a_k[:, c:], 0.0)
        n_ab = jnp.where(strict[None], a_b[:, :c], 0.0)
        a_rb = jnp.where(incl[None], a_b[:, c:], 0.0)
        w_m = x_s[:, :c] + _nn(a_ak, vh)
        t_m = eye + n_ab
        p_m = n_ab
        for _ in range(int(math.log2(c)) - 1):
            p_m = _nn(p_m, p_m)
            t_m = t_m + _nn(t_m, p_m)
        u = _nn(t_m, w_m)
        y = x_s[:, c:] + _nn(a_rk, vh) + _nn(a_rb, u)
        s_scr[...] = (s + _tn(vh, kt) + _tn(u, bt)) * g_last
        mu = jnp.mean(y, axis=-1, keepdims=True)
        dev = y - mu
        var = jnp.mean(dev * dev, axis=-1, keepdims=True)
        yn = _merge_heads(dev * lax.rsqrt(var + GN_EPS), bb) * lnw_ref[...] + lnb_ref[...]
        gate = g_ref[:, sl, :] * _silu(z_ref[0, :, sl, :])
        y_ref[:, sl, :] = (yn + _merge_heads(bonus * vh, bb)) * gate
        return carry

    lax.fori_loop(0, n_chunks, body, 0)

    @pl.when(tb == pl.num_programs(1) - 1)
    def _():
        sfin_ref[...] = s_scr[...].reshape(bb, C_HEADS, C_HEAD, C_HEAD)


def _rwkv_mixer(rkvz, lw, al, g, vmix, v_first, vecs, s0, *, bb, tblock, chunk, t_valid):
    _, bsz, t, d = rkvz.shape
    vres = vmix is not None
    tok = pl.BlockSpec((bb, tblock, d), lambda i, j: (i, j, 0))
    stk = lambda p: pl.BlockSpec((1, bb, tblock, d), lambda i, j: (p, i, j, 0))
    vec = pl.BlockSpec((1, d), lambda i, j: (0, 0))
    st = pl.BlockSpec((bb, C_HEADS, C_HEAD, C_HEAD), lambda i, j: (i, 0, 0, 0))
    in_specs = [stk(0), stk(1), stk(2), stk(3), tok, tok, tok]
    args = [rkvz, rkvz, rkvz, rkvz, lw, al, g]
    if vres:
        in_specs += [tok, stk(2)]
        args += [vmix, v_first]
    in_specs += [vec] * 5 + [st]
    args += [*vecs, s0]
    return pl.pallas_call(
        functools.partial(_rwkv_kernel, bb=bb, chunk=chunk, n_chunks=tblock // chunk, t_valid=t_valid, vres=vres),
        grid=(bsz // bb, t // tblock),
        in_specs=in_specs, out_specs=[tok, st],
        out_shape=[jax.ShapeDtypeStruct((bsz, t, d), F32), jax.ShapeDtypeStruct(s0.shape, F32)],
        scratch_shapes=[pltpu.VMEM((bb * C_HEADS, C_HEAD, C_HEAD), F32)],
        compiler_params=_params(("parallel", "arbitrary")), name="rwkv7_mixer",
    )(*args)


def _s5_params(w, j):
    lr = jnp.minimum(w["ssm_lambda_re"][j], -1e-4)
    li = w["ssm_lambda_im"][j]
    step = jnp.exp(w["ssm_log_step"][j])[:, None]
    mag = jnp.exp(lr * step)
    ab_re = mag * jnp.cos(li * step)
    ab_im = mag * jnp.sin(li * step)
    den = lr * lr + li * li
    nr = ab_re - 1.0
    cr = (nr * lr + ab_im * li) / den
    ci = (ab_im * lr - nr * li) / den
    b_re, b_im = w["ssm_b_re"][j], w["ssm_b_im"][j]
    bb_re = cr[..., None] * b_re - ci[..., None] * b_im
    bb_im = cr[..., None] * b_im + ci[..., None] * b_re
    gpb = A_GROUPS // A_BLOCKS
    eye = jnp.eye(gpb, dtype=F32)

    def pack_in(bb):
        bb = bb.reshape(A_BLOCKS, gpb, A_STATE, A_GROUP)
        return jnp.einsum("bgph,gk->bghkp", bb, eye).reshape(A_BLOCKS, gpb * A_GROUP, gpb * A_STATE)

    def pack_out(c):
        c = c.reshape(A_BLOCKS, gpb, A_GROUP, A_STATE)
        return jnp.einsum("bghp,gk->bgpkh", c, eye).reshape(A_BLOCKS, gpb * A_STATE, gpb * A_GROUP)

    return dict(ab_re=ab_re.reshape(1, A_LANES), ab_im=ab_im.reshape(1, A_LANES),
                bb_re=pack_in(bb_re).astype(BF16), bb_im=pack_in(bb_im).astype(BF16),
                c_re=pack_out(w["ssm_c_re"][j]).astype(BF16), c_im=pack_out(w["ssm_c_im"][j]).astype(BF16),
                d=w["ssm_d"][j].reshape(1, A_WIDTH), glu_w=w["ssm_glu_w"][j],
                glu_b=w["ssm_glu_b"][j].reshape(1, A_WIDTH))


def _trunk(x, ssm_re, ssm_im, hgrn, wkv, shift, w, cfg):
    bsz, t, d = x.shape
    t_valid = cfg["t_valid"]
    m = bsz * t
    h = x.reshape(m, d)
    n_re, n_im, n_hg, n_wkv, n_sh = [], [], [], [], []
    lbs = jax.nn.softmax(w["hgrn_lower_bounds"], axis=0)
    lbs = jnp.cumsum(lbs, axis=0) - lbs[0]
    v_first = None
    y = None
    depth = w["norm_w"].shape[0]
    for layer in range(depth):
        j = layer // 2
        nw = w["norm_w"][layer].reshape(1, d)
        last = layer == depth - 1
        final_nw = w["final_norm_w"].reshape(1, d) if last else None
        if layer % 2 == 0:
            proj = _norm_matmul(h, nw, w["even_w_in"][j], tm=cfg["tm"], tn=512).reshape(bsz, t, -1)
            out_a, hr, hi = _s5_mixer(proj, ssm_re[j].reshape(bsz, A_LANES), ssm_im[j].reshape(bsz, A_LANES),
                                      _s5_params(w, j), bb=cfg["s5_bb"], tc=cfg["s5_tc"], t_valid=t_valid)
            out_b, sb = _hgrn_mixer(proj, hgrn[j], lbs[j].reshape(1, B_WIDTH), w["hgrn_norm_w"][j].reshape(1, B_DIM),
                                    bb=cfg["hgrn_bb"], tblock=cfg["hgrn_tblock"], chunk=cfg["chunk"],
                                    t_valid=t_valid, use_lb=j > 0)
            res = _proj_residual(h, [out_a.reshape(m, A_WIDTH), out_b.reshape(m, B_WIDTH)], w["even_w_out"][j],
                                 final_nw, tm=cfg["tm"])
            n_re.append(hr.reshape(bsz, A_GROUPS, A_STATE))
            n_im.append(hi.reshape(bsz, A_GROUPS, A_STATE))
            n_hg.append(sb)
        else:
            xn, xx = _norm_shift(h.reshape(bsz, t, d), nw, shift[j], bb=cfg["bb"], tblock=cfg["tblock"])
            n_sh.append(xn[:, (t if t_valid is None else t_valid) - 1])
            xn2, xx2 = xn.reshape(m, d), xx.reshape(m, d)
            mix = w["rw_mix"][j]
            rkvz = _mix_proj(xn2, xx2, mix[jnp.array([0, 2, 3, 5])], w["rw_w_rkvz"][j], tm=cfg["tm"], tn=512)
            rkvz = rkvz.reshape(4, bsz, t, d)
            vres = v_first is not None
            row = lambda a: a.reshape(1, -1)
            lora_w = [row(w["rw_w0"][j]), w["rw_w1"][j], w["rw_w2"][j], row(w["rw_a0"][j]), w["rw_a1"][j],
                      w["rw_a2"][j], w["rw_g1"][j], w["rw_g2"][j]]
            if vres:
                lora_w += [row(w["rw_v0"][j - 1]), w["rw_v1"][j - 1], w["rw_v2"][j - 1]]
            outs = _lora_paths(xn2, xx2, mix[jnp.array([1, 4, 3, 5])], lora_w, tm=cfg["tm"], vres=vres)
            lw, al, g = (a.reshape(bsz, t, d) for a in outs[:3])
            vmix = outs[3].reshape(bsz, t, d) if vres else None
            vecs = [row(w[n][j]) for n in ("rw_k_k", "rw_k_a", "rw_r_k", "rw_ln_w", "rw_ln_b")]
            y_mix, sw = _rwkv_mixer(rkvz, lw, al, g, vmix, v_first, vecs, wkv[j], bb=cfg["bb"],
                                    tblock=cfg["tblock"], chunk=cfg["rwkv_chunk"], t_valid=t_valid)
            if v_first is None:
                v_first = rkvz
            res = _proj_residual(h, [y_mix.reshape(m, d)], w["rw_w_o"][j], final_nw, tm=cfg["tm"])
            n_wkv.append(sw)
        if last:
            h, y = res
        else:
            h = res[0]
    return (y.reshape(bsz, t, d), jnp.stack(n_re), jnp.stack(n_im), jnp.stack(n_hg), jnp.stack(n_wkv),
            jnp.stack(n_sh))


MATMUL_WEIGHTS = ("even_w_in", "even_w_out", "ssm_glu_w", "rw_w_rkvz", "rw_w1", "rw_w2", "rw_a1", "rw_a2",
                  "rw_v1", "rw_v2", "rw_g1", "rw_g2", "rw_w_o")
PROMPT_CFG = dict(t_valid=None, tm=512, s5_bb=8, s5_tc=64, bb=1, tblock=256, chunk=16, rwkv_chunk=64,
                  hgrn_bb=8, hgrn_tblock=128)
SAMPLE_PAD = 8
SAMPLE_CFG = dict(t_valid=4, tm=512, s5_bb=32, s5_tc=SAMPLE_PAD, bb=8, tblock=SAMPLE_PAD, chunk=SAMPLE_PAD,
                  rwkv_chunk=SAMPLE_PAD, hgrn_bb=8, hgrn_tblock=SAMPLE_PAD)


def kernel(x_prompt, x_sample, state_ssm_re, state_ssm_im, state_hgrn, state_wkv, state_shift, norm_w, final_norm_w, even_w_in, even_w_out, ssm_lambda_re, ssm_lambda_im, ssm_log_step, ssm_b_re, ssm_b_im, ssm_c_re, ssm_c_im, ssm_d, ssm_glu_w, ssm_glu_b, hgrn_lower_bounds, hgrn_norm_w, rw_mix, rw_w_rkvz, rw_w0, rw_w1, rw_w2, rw_a0, rw_a1, rw_a2, rw_v0, rw_v1, rw_v2, rw_g1, rw_g2, rw_k_k, rw_k_a, rw_r_k, rw_ln_w, rw_ln_b, rw_w_o):
    w = dict(norm_w=norm_w, final_norm_w=final_norm_w, even_w_in=even_w_in, even_w_out=even_w_out,
             ssm_lambda_re=ssm_lambda_re, ssm_lambda_im=ssm_lambda_im, ssm_log_step=ssm_log_step,
             ssm_b_re=ssm_b_re, ssm_b_im=ssm_b_im, ssm_c_re=ssm_c_re, ssm_c_im=ssm_c_im, ssm_d=ssm_d,
             ssm_glu_w=ssm_glu_w, ssm_glu_b=ssm_glu_b, hgrn_lower_bounds=hgrn_lower_bounds,
             hgrn_norm_w=hgrn_norm_w, rw_mix=rw_mix, rw_w_rkvz=rw_w_rkvz, rw_w0=rw_w0, rw_w1=rw_w1,
             rw_w2=rw_w2, rw_a0=rw_a0, rw_a1=rw_a1, rw_a2=rw_a2, rw_v0=rw_v0, rw_v1=rw_v1, rw_v2=rw_v2,
             rw_g1=rw_g1, rw_g2=rw_g2, rw_k_k=rw_k_k, rw_k_a=rw_k_a, rw_r_k=rw_r_k,
             rw_ln_w=rw_ln_w, rw_ln_b=rw_ln_b, rw_w_o=rw_w_o)
    for name in MATMUL_WEIGHTS:
        w[name] = w[name].astype(BF16)
    n_even, n_odd = state_hgrn.shape[0], state_wkv.shape[0]
    bp, bs, ts = x_prompt.shape[0], x_sample.shape[0], x_sample.shape[1]
    assert ts == SAMPLE_CFG["t_valid"]
    z_re = jnp.zeros((n_even, bp, A_GROUPS, A_STATE), F32)
    z_hg = jnp.zeros((n_even, bp, B_HEADS, B_DIM, B_DIM), F32)
    z_wkv = jnp.zeros((n_odd, bp, C_HEADS, C_HEAD, C_HEAD), F32)
    z_sh = jnp.zeros((n_odd, bp, D_MODEL), F32)
    prompt = _trunk(x_prompt, z_re, z_re, z_hg, z_wkv, z_sh, w, PROMPT_CFG)
    xs = jnp.pad(x_sample, ((0, 0), (0, SAMPLE_PAD - ts), (0, 0)))
    sample = _trunk(xs, state_ssm_re, state_ssm_im, state_hgrn, state_wkv, state_shift, w, SAMPLE_CFG)
    y_sample = sample[0][:, :ts]
    return (prompt[0], y_sample, *prompt[1:], *sample[1:])
```

```python
import functools
import math

import jax
import jax.numpy as jnp
from jax import lax
from jax.experimental import pallas as pl
from jax.experimental.pallas import tpu as pltpu

F32 = jnp.float32
BF16 = jnp.bfloat16

D_MODEL = 1024
A_WIDTH = 512
A_GROUP = 16
A_GROUPS = 32
A_STATE = 64
A_LANES = A_GROUPS * A_STATE
A_BLOCKS = 4
B_WIDTH = 512
B_HEADS = 4
B_DIM = 128
C_HEADS = 16
C_HEAD = 64
C_PACK = 4
DECAY_SCALE = math.exp(-0.5)
RMS_EPS = 1e-6
GN_EPS = 64e-5
VMEM_LIMIT = 56 * 1024 * 1024
LANES = 128


def _dot(a, b):
    return jnp.dot(a.astype(b.dtype), b, preferred_element_type=F32)


def _bdot(a, b, contract):
    return lax.dot_general(a, b, (contract, ((0,), (0,))), preferred_element_type=F32)


def _nn(a, b):
    return _bdot(a, b, ((2,), (1,)))


def _nt(a, b):
    return _bdot(a, b, ((2,), (2,)))


def _tn(a, b):
    return _bdot(a, b, ((1,), (1,)))


def _rms(x, w):
    return x * lax.rsqrt(jnp.mean(x * x, axis=-1, keepdims=True) + RMS_EPS) * w


def _sigmoid(x):
    return 1.0 / (1.0 + jnp.exp(-x))


def _silu(x):
    return x * _sigmoid(x)


def _params(sem):
    return pltpu.CompilerParams(dimension_semantics=sem, vmem_limit_bytes=VMEM_LIMIT)


def _stacked_state_specs(prev, state_shape, n_layers, layer, bb):
    tail = (0,) * (len(state_shape) - 1)
    shape = jax.ShapeDtypeStruct((n_layers, *state_shape), F32)
    block = (bb, *state_shape[1:])
    if prev is None:
        return pl.BlockSpec((n_layers, *block), lambda i, j: (0, i, *tail)), shape, layer, False
    return pl.BlockSpec((1, *block), lambda i, j: (layer, i, *tail)), shape, 0, True


def _write_state_slab(ref, slot, aliased, value):
    if not aliased:
        for other in range(ref.shape[0]):
            if other != slot:
                ref[other] = jnp.zeros(ref.shape[1:], F32)
    ref[slot] = value


def _tri_masks(c):
    row = lax.broadcasted_iota(jnp.int32, (c, c), 0)
    col = lax.broadcasted_iota(jnp.int32, (c, c), 1)
    return row >= col, row > col, row == col


def _split_heads(x, width):
    bb, _, d = x.shape
    return jnp.stack([x[i, :, h * width:(h + 1) * width] for i in range(bb) for h in range(d // width)], axis=0)


def _merge_heads(x, bb):
    h = x.shape[0] // bb
    return jnp.stack([jnp.concatenate([x[i * h + j] for j in range(h)], axis=-1) for i in range(bb)], axis=0)


def _cumsum_time(tri_f, x):
    hi = x.astype(BF16).astype(F32)
    rest = x - hi
    mid = rest.astype(BF16).astype(F32)
    lo = rest - mid
    return jnp.stack([sum(jnp.dot(tri_f, p[i], preferred_element_type=F32) for p in (hi, mid, lo))
                      for i in range(x.shape[0])], axis=0)


def _norm_matmul_kernel(x_ref, nw_ref, w_ref, o_ref):
    o_ref[...] = _dot(_rms(x_ref[...], nw_ref[...]), w_ref[...])


def _norm_matmul(x, nw, w, *, tm, tn):
    m, k = x.shape
    n = w.shape[1]
    return pl.pallas_call(
        _norm_matmul_kernel, grid=(m // tm, n // tn),
        in_specs=[pl.BlockSpec((tm, k), lambda i, j: (i, 0)),
                  pl.BlockSpec((1, k), lambda i, j: (0, 0)),
                  pl.BlockSpec((k, tn), lambda i, j: (0, j))],
        out_specs=pl.BlockSpec((tm, tn), lambda i, j: (i, j)),
        out_shape=jax.ShapeDtypeStruct((m, n), F32),
        compiler_params=_params(("parallel", "arbitrary")), name="norm_in_proj",
    )(x, nw, w)


def _proj_residual_kernel(*refs, n_x, final_norm):
    h_ref, x_refs, w_ref = refs[0], refs[1:1 + n_x], refs[1 + n_x]
    x = jnp.concatenate([r[...] for r in x_refs], axis=-1) if n_x > 1 else x_refs[0][...]
    h = h_ref[...] + _dot(x, w_ref[...])
    if final_norm:
        nw_ref, h_out, y_out = refs[2 + n_x:]
        y_out[...] = _rms(h, nw_ref[...])
    else:
        h_out = refs[2 + n_x]
    h_out[...] = h


def _proj_residual(h, xs, w, final_nw=None, *, tm):
    m, n = h.shape
    final_norm = final_nw is not None
    row = lambda width: pl.BlockSpec((tm, width), lambda i: (i, 0))
    in_specs = [row(n)] + [row(x.shape[1]) for x in xs] + [pl.BlockSpec(w.shape, lambda i: (0, 0))]
    args = [h, *xs, w]
    out_specs, out_shape = [row(n)], [jax.ShapeDtypeStruct((m, n), F32)]
    if final_norm:
        in_specs.append(pl.BlockSpec((1, n), lambda i: (0, 0)))
        args.append(final_nw)
        out_specs.append(row(n))
        out_shape.append(jax.ShapeDtypeStruct((m, n), F32))
    return pl.pallas_call(
        functools.partial(_proj_residual_kernel, n_x=len(xs), final_norm=final_norm), grid=(m // tm,),
        in_specs=in_specs, out_specs=out_specs, out_shape=out_shape,
        compiler_params=_params(("parallel",)), name="out_proj_residual",
    )(*args)


def _s5_kernel(u_ref, za_ref, h0r_ref, h0i_ref, ar_ref, ai_ref, bbr_ref, bbi_ref, cr_ref, ci_ref,
               d_ref, gw_ref, gb_ref, out_ref, hfr_ref, hfi_ref, hre, him, bur, bui, st_r, st_i, wbm, wtm,
               *, bsz, tc, n_steps):
    i = pl.program_id(1)
    rows = bsz * tc
    n_tiles = A_LANES // LANES
    u_tiles = A_WIDTH // LANES
    per = n_tiles // A_BLOCKS
    assert A_WIDTH // A_BLOCKS == LANES

    @pl.when(i == 0)
    def _():
        for cb in range(n_tiles):
            st_r[cb] = h0r_ref[:, cb * LANES:(cb + 1) * LANES]
            st_i[cb] = h0i_ref[:, cb * LANES:(cb + 1) * LANES]

    def bm(t):
        return pl.ds(t, bsz, stride=tc)

    def tm(t):
        return pl.ds(pl.multiple_of(t * bsz, bsz), bsz)

    def reorder(src, dst, src_rows, dst_rows):
        def body(t, carry):
            for q in range(u_tiles):
                dst.at[q][dst_rows(t), :] = src.at[q][src_rows(t), :]
            return carry
        lax.fori_loop(0, tc, body, 0)

    u_bm = u_ref[...].reshape(rows, A_WIDTH)
    for q in range(u_tiles):
        wbm[q] = u_bm[:, q * LANES:(q + 1) * LANES]
    reorder(wbm, wtm, bm, tm)

    for j in range(A_BLOCKS):
        bu_r = _dot(wtm[j], bbr_ref[j])
        bu_i = _dot(wtm[j], bbi_ref[j])
        for q in range(per):
            bur[j * per + q] = bu_r[:, q * LANES:(q + 1) * LANES]
            bui[j * per + q] = bu_i[:, q * LANES:(q + 1) * LANES]

    def advance(t, prev):
        new = []
        for cb in range(n_tiles):
            lanes = slice(cb * LANES, (cb + 1) * LANES)
            ar, ai = ar_ref[:, lanes], ai_ref[:, lanes]
            pr, pi = prev[2 * cb], prev[2 * cb + 1]
            hr = ar * pr - ai * pi + bur[cb, tm(t), :]
            hi = ar * pi + ai * pr + bui[cb, tm(t), :]
            hre[cb, tm(t), :] = hr
            him[cb, tm(t), :] = hi
            new += [hr, hi]
        return tuple(new)

    state = tuple(ref[cb] for cb in range(n_tiles) for ref in (st_r, st_i))
    if n_steps <= 8:
        for t in range(n_steps):
            state = advance(t, state)
    else:
        state = lax.fori_loop(0, n_steps, advance, state)
    for cb in range(n_tiles):
        st_r[cb] = state[2 * cb]
        st_i[cb] = state[2 * cb + 1]
    if n_steps < tc:
        pad = pl.ds(n_steps * bsz, (tc - n_steps) * bsz)
        for cb in range(n_tiles):
            hre[cb, pad, :] = bur[cb, pad, :]
            him[cb, pad, :] = bui[cb, pad, :]

    def block(ref, j):
        return jnp.concatenate([ref[j * per + q] for q in range(per)], axis=-1)

    u_tm = jnp.concatenate([wtm[q] for q in range(u_tiles)], axis=-1)
    y = jnp.concatenate(
        [_dot(block(hre, j), cr_ref[j]) - _dot(block(him, j), ci_ref[j]) for j in range(A_BLOCKS)],
        axis=-1) + d_ref[...] * u_tm
    y = 0.5 * y * (1.0 + jnp.tanh(math.sqrt(2.0 / math.pi) * (y + 0.044715 * (y * y * y))))
    y = y * _sigmoid(_dot(y, gw_ref[...]) + gb_ref[...])
    for q in range(u_tiles):
        wtm[q] = y[:, q * LANES:(q + 1) * LANES]
    reorder(wtm, wbm, tm, bm)
    y_bm = jnp.concatenate([wbm[q] for q in range(u_tiles)], axis=-1)
    out = y_bm * _silu(za_ref[...].reshape(rows, A_WIDTH))
    out_ref[...] = out.reshape(bsz, tc, A_WIDTH)

    @pl.when(i == pl.num_programs(1) - 1)
    def _():
        for cb in range(n_tiles):
            hfr_ref[:, cb * LANES:(cb + 1) * LANES] = st_r[cb]
            hfi_ref[:, cb * LANES:(cb + 1) * LANES] = st_i[cb]


def _s5_mixer(proj, h0r, h0i, sp, *, bb, tc, t_valid):
    bsz, t, _ = proj.shape
    n_steps = tc if t_valid is None else t_valid
    full = lambda a: pl.BlockSpec(a.shape, lambda b, i: (0,) * a.ndim)
    col = lambda c: pl.BlockSpec((bb, tc, A_WIDTH), lambda b, i: (b, i, c))
    st = pl.BlockSpec((bb, A_LANES), lambda b, i: (b, 0))
    weights = [sp["ab_re"], sp["ab_im"], sp["bb_re"], sp["bb_im"], sp["c_re"], sp["c_im"],
               sp["d"], sp["glu_w"], sp["glu_b"]]
    state = jax.ShapeDtypeStruct((bsz, A_LANES), F32)
    return pl.pallas_call(
        functools.partial(_s5_kernel, bsz=bb, tc=tc, n_steps=n_steps), grid=(bsz // bb, t // tc),
        in_specs=[col(0), col(1), st, st] + [full(a) for a in weights],
        out_specs=[col(0), st, st],
        out_shape=[jax.ShapeDtypeStruct((bsz, t, A_WIDTH), F32), state, state],
        scratch_shapes=[pltpu.VMEM((A_LANES // LANES, bb * tc, LANES), F32)] * 4
        + [pltpu.VMEM((A_LANES // LANES, bb, LANES), F32)] * 2
        + [pltpu.VMEM((A_WIDTH // LANES, bb * tc, LANES), F32)] * 2,
        compiler_params=_params(("parallel", "arbitrary")), name="s5_mixer",
    )(proj, proj, h0r, h0i, *weights)


def _hgrn_kernel(*refs, bb, chunk, n_chunks, t_valid, use_lb, slot, aliased):
    q_ref, f_ref, i_ref, zb_ref, lb_ref, nw_ref, s0_ref = refs[:7]
    out_ref, sfin_ref, s_scr = refs[-3:]
    c = chunk
    tb = pl.program_id(1)

    @pl.when(tb == 0)
    def _():
        s_scr[...] = jnp.swapaxes(s0_ref[...].reshape(bb * B_HEADS, B_DIM, B_DIM), 1, 2)

    incl, _, _ = _tri_masks(c)
    tri_f = incl.astype(F32)

    def body(ci, carry):
        off = pl.multiple_of(ci * c, c)
        sl = pl.ds(off, c)
        q = _silu(q_ref[:, sl, :])
        f = f_ref[:, sl, :]
        if use_lb:
            lb = lb_ref[...]
            log_f = jnp.log(lb + (1.0 - lb) * _sigmoid(f))
            k = (1.0 - lb) * _sigmoid(-f)
        else:
            log_f = jnp.minimum(f, 0.0) - jnp.log1p(jnp.exp(-jnp.abs(f)))
            k = _sigmoid(-f)
        if t_valid is not None:
            tok = tb * (n_chunks * c) + off + lax.broadcasted_iota(jnp.int32, (1, c, 1), 1)
            log_f = jnp.where(tok < t_valid, log_f, 0.0)
            k = jnp.where(tok < t_valid, k, 0.0)
        cum = _cumsum_time(tri_f, log_f)
        last = cum[:, c - 1:c, :]
        qh = _split_heads(q * jnp.exp(cum), B_DIM)
        kh = _split_heads(k * jnp.exp(-cum), B_DIM)
        ke = _split_heads(k * jnp.exp(last - cum), B_DIM)
        vh = _split_heads(i_ref[:, sl, :], B_DIM)
        gl = _split_heads(jnp.exp(last), B_DIM)
        s = s_scr[...]
        att = jnp.where(incl[None], _nt(qh, kh), 0.0)
        o = _nn(att, vh) + _nt(qh, s)
        s_scr[...] = s * gl + _tn(vh, ke)
        o = o * lax.rsqrt(jnp.mean(o * o, axis=-1, keepdims=True) + RMS_EPS) * nw_ref[...]
        out_ref[:, sl, :] = _merge_heads(o, bb) * _silu(zb_ref[:, sl, :])
        return carry

    lax.fori_loop(0, n_chunks, body, 0)

    @pl.when(tb == pl.num_programs(1) - 1)
    def _():
        _write_state_slab(sfin_ref, slot, aliased,
                          jnp.swapaxes(s_scr[...], 1, 2).reshape(bb, B_HEADS, B_DIM, B_DIM))


def _hgrn_mixer(proj, s0, lb, nw, prev, *, layer, n_layers, bb, tblock, chunk, t_valid, use_lb):
    bsz, t, _ = proj.shape
    st_out, st_shape, slot, aliased = _stacked_state_specs(prev, s0.shape, n_layers, layer, bb)
    extra = ([pl.BlockSpec(memory_space=pl.ANY)], [prev], {7: 1}) if aliased else ([], [], {})
    col = lambda c: pl.BlockSpec((bb, tblock, B_WIDTH), lambda i, j: (i, j, c))
    vec = lambda a: pl.BlockSpec(a.shape, lambda i, j: (0, 0))
    st = pl.BlockSpec((bb, B_HEADS, B_DIM, B_DIM), lambda i, j: (i, 0, 0, 0))
    return pl.pallas_call(
        functools.partial(_hgrn_kernel, bb=bb, chunk=chunk, n_chunks=tblock // chunk, t_valid=t_valid,
                          use_lb=use_lb, slot=slot, aliased=aliased),
        grid=(bsz // bb, t // tblock),
        in_specs=[col(2), col(3), col(4), col(5), vec(lb), vec(nw), st] + extra[0],
        out_specs=[pl.BlockSpec((bb, tblock, B_WIDTH), lambda i, j: (i, j, 0)), st_out],
        out_shape=[jax.ShapeDtypeStruct((bsz, t, B_WIDTH), F32), st_shape],
        scratch_shapes=[pltpu.VMEM((bb * B_HEADS, B_DIM, B_DIM), F32)],
        input_output_aliases=extra[2],
        compiler_params=_params(("parallel", "arbitrary")), name="hgrn2_mixer",
    )(proj, proj, proj, proj, lb, nw, s0, *extra[1])


def _shift_kernel(h_ref, nw_ref, sh_ref, xn_ref, xx_ref, carry, *, tblock):
    @pl.when(pl.program_id(1) == 0)
    def _():
        carry[...] = sh_ref[...]

    xn = _rms(h_ref[...], nw_ref[...])
    row = lax.broadcasted_iota(jnp.int32, (1, tblock, 1), 1)
    x_prev = jnp.where(row == 0, carry[...], pltpu.roll(xn, 1, axis=1))
    xn_ref[...] = xn
    xx_ref[...] = x_prev - xn
    carry[...] = xn[:, tblock - 1:tblock, :]


def _norm_shift(h, nw, shift, *, bb, tblock):
    bsz, t, d = h.shape
    tok = pl.BlockSpec((bb, tblock, d), lambda i, j: (i, j, 0))
    out = jax.ShapeDtypeStruct((bsz, t, d), F32)
    return pl.pallas_call(
        functools.partial(_shift_kernel, tblock=tblock), grid=(bsz // bb, t // tblock),
        in_specs=[tok, pl.BlockSpec((1, d), lambda i, j: (0, 0)), pl.BlockSpec((bb, 1, d), lambda i, j: (i, 0, 0))],
        out_specs=[tok, tok], out_shape=[out, out],
        scratch_shapes=[pltpu.VMEM((bb, 1, d), F32)],
        compiler_params=_params(("parallel", "arbitrary")), name="norm_token_shift",
    )(h, nw, shift.reshape(bsz, 1, d))


def _mixproj_kernel(xn_ref, xx_ref, mix_ref, w_ref, o_ref):
    o_ref[0] = _dot(xn_ref[...] + xx_ref[...] * mix_ref[0], w_ref[0])


def _mix_proj(xn, xx, mix, w, *, tm, tn):
    m, k = xn.shape
    p, _, n = w.shape
    tok = pl.BlockSpec((tm, k), lambda i, q, j: (i, 0))
    return pl.pallas_call(
        _mixproj_kernel, grid=(m // tm, p, n // tn),
        in_specs=[tok, tok, pl.BlockSpec((1, 1, k), lambda i, q, j: (q, 0, 0)),
                  pl.BlockSpec((1, k, tn), lambda i, q, j: (q, 0, j))],
        out_specs=pl.BlockSpec((1, tm, tn), lambda i, q, j: (q, i, j)),
        out_shape=jax.ShapeDtypeStruct((p, m, n), F32),
        compiler_params=_params(("parallel", "arbitrary", "arbitrary")), name="rwkv_rkvz_proj",
    )(xn, xx, mix.reshape(p, 1, k), w)


def _lora_kernel(*refs, vres):
    xn_ref, xx_ref, mix_ref, w0, w1, w2, a0, a1, a2, g1, g2 = refs[:11]
    rest = refs[11:]
    xn, xx = xn_ref[...], xx_ref[...]
    mixed = lambda m: xn + xx * mix_ref[m:m + 1, :]
    if vres:
        v0, v1, v2, lw_ref, al_ref, g_ref, vm_ref = rest
        vm_ref[...] = _sigmoid(v0[...] + _dot(_dot(mixed(2), v1[...]), v2[...]))
    else:
        lw_ref, al_ref, g_ref = rest
    lw_ref[...] = -DECAY_SCALE * _sigmoid(w0[...] + _dot(jnp.tanh(_dot(mixed(0), w1[...])), w2[...]))
    al_ref[...] = _sigmoid(a0[...] + _dot(_dot(mixed(1), a1[...]), a2[...]))
    g_ref[...] = _dot(_sigmoid(_dot(mixed(3), g1[...])), g2[...])


def _lora_paths(xn, xx, mix, weights, *, tm, vres):
    m, d = xn.shape
    tok = pl.BlockSpec((tm, d), lambda i: (i, 0))
    full = lambda a: pl.BlockSpec(a.shape, lambda i: (0, 0))
    n_out = 4 if vres else 3
    return pl.pallas_call(
        functools.partial(_lora_kernel, vres=vres), grid=(m // tm,),
        in_specs=[tok, tok, full(mix)] + [full(a) for a in weights],
        out_specs=[tok] * n_out, out_shape=[jax.ShapeDtypeStruct((m, d), F32)] * n_out,
        compiler_params=_params(("parallel",)), name="rwkv_lora_paths",
    )(xn, xx, mix, *weights)


def _rwkv_kernel(*refs, bb, chunk, n_chunks, t_valid, vres, slot, aliased):
    r_ref, k_ref, v_ref, z_ref, lw_ref, al_ref, g_ref = refs[:7]
    if vres:
        vm_ref, vf_ref = refs[7:9]
        refs = refs[9:]
    else:
        refs = refs[7:]
    kk_ref, ka_ref, rk_ref, lnw_ref, lnb_ref, seg_ref, s0_ref = refs[:7]
    y_ref, sfin_ref, s_scr = refs[-3:]
    c = chunk
    gw = C_PACK * C_HEAD
    aw = C_PACK * c
    n_groups = D_MODEL // gw
    n_sq = int(math.log2(c)) - 1
    tb = pl.program_id(1)

    @pl.when(tb == 0)
    def _():
        for b in range(bb):
            s_scr[b] = jnp.concatenate([s0_ref[b, h] for h in range(C_HEADS)], axis=-1)

    row = lax.broadcasted_iota(jnp.int32, (c, aw), 0)
    col = lax.broadcasted_iota(jnp.int32, (c, aw), 1) % c
    incl, strict = row >= col, row > col
    eye = (row == col).astype(F32)
    tri_f = _tri_masks(c)[0].astype(F32)

    def bd_mask(rows_per_head, lanes, lanes_per_head):
        r = lax.broadcasted_iota(jnp.int32, (C_PACK * rows_per_head, lanes), 0) // rows_per_head
        l = lax.broadcasted_iota(jnp.int32, (C_PACK * rows_per_head, lanes), 1) // lanes_per_head
        return r == l

    bd_c, bd_v, bd_p = bd_mask(c, gw, C_HEAD), bd_mask(C_HEAD, gw, C_HEAD), bd_mask(c, aw, c)
    lane_head = lax.broadcasted_iota(jnp.int32, (1, gw), 1) // C_HEAD
    seg = seg_ref[...]

    def bd(y, mask):
        return jnp.where(mask, jnp.tile(y, (C_PACK, 1)), 0.0)

    def seg_sum(x):
        hi = x.astype(BF16).astype(F32)
        return jnp.dot(hi, seg, preferred_element_type=F32) + jnp.dot(x - hi, seg, preferred_element_type=F32)

    def nn(a, b):
        return jnp.dot(a, b, preferred_element_type=F32)

    def nt(a, b):
        return lax.dot_general(a, b, (((1,), (1,)), ((), ())), preferred_element_type=F32)

    def tn(a, b):
        return lax.dot_general(a, b, (((0,), (0,)), ((), ())), preferred_element_type=F32)

    def cat(a, b):
        return jnp.concatenate([a, b], axis=0)

    def body(ci, carry):
        off = pl.multiple_of(ci * c, c)
        sl = pl.ds(off, c)
        inst = [(b, slice(g * gw, (g + 1) * gw)) for b in range(bb) for g in range(n_groups)]
        r_l, k_l, v_l, kk_l, al_l, rk_l, gi_l, gv_l, gm_l = ([] for _ in range(9))
        for b in range(bb):
            r = r_ref[0, b, sl, :]
            k = k_ref[0, b, sl, :]
            v = v_ref[0, b, sl, :]
            lw = lw_ref[b, sl, :]
            al = al_ref[b, sl, :]
            if vres:
                v = v + (vf_ref[0, b, sl, :] - v) * vm_ref[b, sl, :]
            kk_raw = k * kk_ref[...]
            k = k * (1.0 + (al - 1.0) * ka_ref[...])
            rk = r * k * rk_ref[...]
            if t_valid is not None:
                ok = tb * (n_chunks * c) + off + lax.broadcasted_iota(jnp.int32, (c, 1), 0) < t_valid
                lw = jnp.where(ok, lw, 0.0)
                k = jnp.where(ok, k, 0.0)
                al = jnp.where(ok, al, 0.0)
            cum = _cumsum_time(tri_f, lw[None])[0]
            g_in, g_inv, g_m = jnp.exp(cum), jnp.exp(-cum), jnp.exp(cum - lw)
            for g in range(n_groups):
                ln = slice(g * gw, (g + 1) * gw)
                for lst, val in ((r_l, r), (k_l, k), (v_l, v), (kk_l, kk_raw), (al_l, al), (rk_l, rk),
                                 (gi_l, g_in), (gv_l, g_inv), (gm_l, g_m)):
                    lst.append(val[:, ln])
        n = len(inst)
        each = range(n)
        ss_l = [seg_sum(kk_l[i] * kk_l[i]) for i in each]
        kk_l = [kk_l[i] / jnp.maximum(jnp.sqrt(ss_l[i]), 1e-12) for i in each]
        kt_l = [k_l[i] * gv_l[i] for i in each]
        bt_l = [kk_l[i] * al_l[i] * gv_l[i] for i in each]
        x_l = [cat(-kk_l[i] * gm_l[i], r_l[i] * gi_l[i]) for i in each]
        s_l = [s_scr[b, :, ln] for b, ln in inst]
        a_b = [nt(x_l[i], bd(bt_l[i], bd_c)) for i in each]
        a_k = [nt(x_l[i], bd(kt_l[i], bd_c)) for i in each]
        x_s = [nt(x_l[i], bd(s_l[i], bd_v)) for i in each]
        n_ab = [jnp.where(strict, a_b[i][:c], 0.0) for i in each]
        a_rb = [jnp.where(incl, a_b[i][c:], 0.0) for i in each]
        a_kk = [cat(jnp.where(strict, a_k[i][:c], 0.0), jnp.where(incl, a_k[i][c:], 0.0)) for i in each]
        a_v = [nn(a_kk[i], bd(v_l[i], bd_c)) for i in each]
        w_m = [x_s[i][:c] + a_v[i][:c] for i in each]
        t_m = [eye + n_ab[i] for i in each]
        p = n_ab
        if n_sq >= 1:
            p = [nn(p[i], bd(p[i], bd_p)) for i in each]
            for _ in range(n_sq - 1):
                both = [nn(cat(p[i], t_m[i]), bd(p[i], bd_p)) for i in each]
                p = [both[i][:c] for i in each]
                t_m = [t_m[i] + both[i][c:] for i in each]
            t_m = [t_m[i] + nn(t_m[i], bd(p[i], bd_p)) for i in each]
        u = [nn(t_m[i], bd(w_m[i], bd_c)) for i in each]
        y = [x_s[i][c:] + a_v[i][c:] for i in each]
        y_u = [nn(a_rb[i], bd(u[i], bd_c)) for i in each]
        full = [tn(cat(v_l[i], u[i]), cat(kt_l[i], bt_l[i])) for i in each]
        for i, (b, ln) in enumerate(inst):
            upd = sum(jnp.where(lane_head == h, full[i][h * C_HEAD:(h + 1) * C_HEAD], 0.0) for h in range(C_PACK))
            s_scr[b, :, ln] = (s_l[i] + upd) * gi_l[i][c - 1:c]
        y = [y[i] + y_u[i] for i in each]
        mu = [seg_sum(y[i]) * (1.0 / C_HEAD) for i in each]
        dev = [y[i] - mu[i] for i in each]
        var = [seg_sum(dev[i] * dev[i]) * (1.0 / C_HEAD) for i in each]
        bonus = [seg_sum(rk_l[i]) * v_l[i] for i in each]
        yn = [dev[i] * lax.rsqrt(var[i] + GN_EPS) for i in each]
        for b in range(bb):
            mine = range(b * n_groups, (b + 1) * n_groups)
            yn_b = jnp.concatenate([yn[i] for i in mine], axis=-1)
            bonus_b = jnp.concatenate([bonus[i] for i in mine], axis=-1)
            gate = g_ref[b, sl, :] * _silu(z_ref[0, b, sl, :])
            y_ref[b, sl, :] = (yn_b * lnw_ref[...] + lnb_ref[...] + bonus_b) * gate
        return carry

    lax.fori_loop(0, n_chunks, body, 0)

    @pl.when(tb == pl.num_programs(1) - 1)
    def _():
        final = jnp.stack([jnp.stack([s_scr[b, :, h * C_HEAD:(h + 1) * C_HEAD] for h in range(C_HEADS)])
                           for b in range(bb)])
        _write_state_slab(sfin_ref, slot, aliased, final)


def _rwkv_mixer(rkvz, lw, al, g, vmix, v_first, vecs, s0, prev, *, layer, n_layers, bb, tblock, chunk, t_valid):
    _, bsz, t, d = rkvz.shape
    st_out, st_shape, slot, aliased = _stacked_state_specs(prev, s0.shape, n_layers, layer, bb)
    vres = vmix is not None
    tok = pl.BlockSpec((bb, tblock, d), lambda i, j: (i, j, 0))
    stk = lambda p: pl.BlockSpec((1, bb, tblock, d), lambda i, j: (p, i, j, 0))
    vec = pl.BlockSpec((1, d), lambda i, j: (0, 0))
    st = pl.BlockSpec((bb, C_HEADS, C_HEAD, C_HEAD), lambda i, j: (i, 0, 0, 0))
    in_specs = [stk(0), stk(1), stk(2), stk(3), tok, tok, tok]
    args = [rkvz, rkvz, rkvz, rkvz, lw, al, g]
    if vres:
        in_specs += [tok, stk(2)]
        args += [vmix, v_first]
    seg = jnp.kron(jnp.eye(C_PACK, dtype=F32), jnp.ones((C_HEAD, C_HEAD), F32))
    in_specs += [vec] * 5 + [pl.BlockSpec(seg.shape, lambda i, j: (0, 0)), st]
    args += [*vecs, seg, s0]
    aliases = {}
    if aliased:
        aliases = {len(args): 1}
        in_specs.append(pl.BlockSpec(memory_space=pl.ANY))
        args.append(prev)
    return pl.pallas_call(
        functools.partial(_rwkv_kernel, bb=bb, chunk=chunk, n_chunks=tblock // chunk, t_valid=t_valid, vres=vres,
                          slot=slot, aliased=aliased),
        grid=(bsz // bb, t // tblock),
        in_specs=in_specs, out_specs=[tok, st_out], out_shape=[jax.ShapeDtypeStruct((bsz, t, d), F32), st_shape],
        input_output_aliases=aliases,
        scratch_shapes=[pltpu.VMEM((bb, C_HEAD, d), F32)],
        compiler_params=_params(("parallel", "arbitrary")), name="rwkv7_mixer",
    )(*args)


def _s5_params(w, j):
    lr = jnp.minimum(w["ssm_lambda_re"][j], -1e-4)
    li = w["ssm_lambda_im"][j]
    step = jnp.exp(w["ssm_log_step"][j])[:, None]
    mag = jnp.exp(lr * step)
    ab_re = mag * jnp.cos(li * step)
    ab_im = mag * jnp.sin(li * step)
    den = lr * lr + li * li
    nr = ab_re - 1.0
    cr = (nr * lr + ab_im * li) / den
    ci = (ab_im * lr - nr * li) / den
    b_re, b_im = w["ssm_b_re"][j], w["ssm_b_im"][j]
    bb_re = cr[..., None] * b_re - ci[..., None] * b_im
    bb_im = cr[..., None] * b_im + ci[..., None] * b_re
    gpb = A_GROUPS // A_BLOCKS
    eye = jnp.eye(gpb, dtype=F32)

    def pack_in(bb):
        bb = bb.reshape(A_BLOCKS, gpb, A_STATE, A_GROUP)
        return jnp.einsum("bgph,gk->bghkp", bb, eye).reshape(A_BLOCKS, gpb * A_GROUP, gpb * A_STATE)

    def pack_out(c):
        c = c.reshape(A_BLOCKS, gpb, A_GROUP, A_STATE)
        return jnp.einsum("bghp,gk->bgpkh", c, eye).reshape(A_BLOCKS, gpb * A_STATE, gpb * A_GROUP)

    return dict(ab_re=ab_re.reshape(1, A_LANES), ab_im=ab_im.reshape(1, A_LANES),
                bb_re=pack_in(bb_re).astype(BF16), bb_im=pack_in(bb_im).astype(BF16),
                c_re=pack_out(w["ssm_c_re"][j]).astype(BF16), c_im=pack_out(w["ssm_c_im"][j]).astype(BF16),
                d=w["ssm_d"][j].reshape(1, A_WIDTH), glu_w=w["ssm_glu_w"][j],
                glu_b=w["ssm_glu_b"][j].reshape(1, A_WIDTH))


def _trunk(x, ssm_re, ssm_im, hgrn, wkv, shift, w, cfg):
    bsz, t, d = x.shape
    t_valid = cfg["t_valid"]
    m = bsz * t
    h = x.reshape(m, d)
    n_re, n_im, n_sh = [], [], []
    hg_all = wkv_all = None
    lbs = jax.nn.softmax(w["hgrn_lower_bounds"], axis=0)
    lbs = jnp.cumsum(lbs, axis=0) - lbs[0]
    v_first = None
    y = None
    depth = w["norm_w"].shape[0]
    for layer in range(depth):
        j = layer // 2
        nw = w["norm_w"][layer].reshape(1, d)
        last = layer == depth - 1
        final_nw = w["final_norm_w"].reshape(1, d) if last else None
        if layer % 2 == 0:
            proj = _norm_matmul(h, nw, w["even_w_in"][j], tm=cfg["tm"], tn=1024).reshape(bsz, t, -1)
            out_a, hr, hi = _s5_mixer(proj, ssm_re[j].reshape(bsz, A_LANES), ssm_im[j].reshape(bsz, A_LANES),
                                      _s5_params(w, j), bb=cfg["s5_bb"], tc=cfg["s5_tc"], t_valid=t_valid)
            out_b, hg_all = _hgrn_mixer(proj, hgrn[j], lbs[j].reshape(1, B_WIDTH),
                                        w["hgrn_norm_w"][j].reshape(1, B_DIM), hg_all, layer=j,
                                        n_layers=hgrn.shape[0], bb=cfg["hgrn_bb"], tblock=cfg["hgrn_tblock"],
                                        chunk=cfg["chunk"], t_valid=t_valid, use_lb=j > 0)
            res = _proj_residual(h, [out_a.reshape(m, A_WIDTH), out_b.reshape(m, B_WIDTH)], w["even_w_out"][j],
                                 final_nw, tm=cfg["tm"])
            n_re.append(hr.reshape(bsz, A_GROUPS, A_STATE))
            n_im.append(hi.reshape(bsz, A_GROUPS, A_STATE))
        else:
            xn, xx = _norm_shift(h.reshape(bsz, t, d), nw, shift[j], bb=cfg["bb"], tblock=cfg["tblock"])
            n_sh.append(xn[:, (t if t_valid is None else t_valid) - 1])
            xn2, xx2 = xn.reshape(m, d), xx.reshape(m, d)
            mix = w["rw_mix"][j]
            rkvz = _mix_proj(xn2, xx2, mix[jnp.array([0, 2, 3, 5])], w["rw_w_rkvz"][j], tm=cfg["tm"], tn=1024)
            rkvz = rkvz.reshape(4, bsz, t, d)
            vres = v_first is not None
            row = lambda a: a.reshape(1, -1)
            lora_w = [row(w["rw_w0"][j]), w["rw_w1"][j], w["rw_w2"][j], row(w["rw_a0"][j]), w["rw_a1"][j],
                      w["rw_a2"][j], w["rw_g1"][j], w["rw_g2"][j]]
            if vres:
                lora_w += [row(w["rw_v0"][j - 1]), w["rw_v1"][j - 1], w["rw_v2"][j - 1]]
            outs = _lora_paths(xn2, xx2, mix[jnp.array([1, 4, 3, 5])], lora_w, tm=cfg["tm"] // 2, vres=vres)
            lw, al, g = (a.reshape(bsz, t, d) for a in outs[:3])
            vmix = outs[3].reshape(bsz, t, d) if vres else None
            vecs = [row(w[n][j]) for n in ("rw_k_k", "rw_k_a", "rw_r_k", "rw_ln_w", "rw_ln_b")]
            y_mix, wkv_all = _rwkv_mixer(rkvz, lw, al, g, vmix, v_first, vecs, wkv[j], wkv_all, layer=j,
                                         n_layers=wkv.shape[0], bb=cfg["bb"], tblock=cfg["tblock"],
                                         chunk=cfg["rwkv_chunk"], t_valid=t_valid)
            if v_first is None:
                v_first = rkvz
            res = _proj_residual(h, [y_mix.reshape(m, d)], w["rw_w_o"][j], final_nw, tm=cfg["tm"])
        if last:
            h, y = res
        else:
            h = res[0]
    return (y.reshape(bsz, t, d), jnp.stack(n_re), jnp.stack(n_im), hg_all, wkv_all,
            jnp.stack(n_sh))


MATMUL_WEIGHTS = ("even_w_in", "even_w_out", "ssm_glu_w", "rw_w_rkvz", "rw_w1", "rw_w2", "rw_a1", "rw_a2",
                  "rw_v1", "rw_v2", "rw_g1", "rw_g2", "rw_w_o")
PROMPT_CFG = dict(t_valid=None, tm=1024, s5_bb=8, s5_tc=64, bb=2, tblock=128, chunk=16, rwkv_chunk=64,
                  hgrn_bb=8, hgrn_tblock=128)
SAMPLE_PAD = 8
SAMPLE_CFG = dict(t_valid=4, tm=1024, s5_bb=32, s5_tc=SAMPLE_PAD, bb=8, tblock=SAMPLE_PAD, chunk=SAMPLE_PAD,
                  rwkv_chunk=SAMPLE_PAD, hgrn_bb=8, hgrn_tblock=SAMPLE_PAD)


def kernel(x_prompt, x_sample, state_ssm_re, state_ssm_im, state_hgrn, state_wkv, state_shift, norm_w, final_norm_w, even_w_in, even_w_out, ssm_lambda_re, ssm_lambda_im, ssm_log_step, ssm_b_re, ssm_b_im, ssm_c_re, ssm_c_im, ssm_d, ssm_glu_w, ssm_glu_b, hgrn_lower_bounds, hgrn_norm_w, rw_mix, rw_w_rkvz, rw_w0, rw_w1, rw_w2, rw_a0, rw_a1, rw_a2, rw_v0, rw_v1, rw_v2, rw_g1, rw_g2, rw_k_k, rw_k_a, rw_r_k, rw_ln_w, rw_ln_b, rw_w_o):
    w = dict(norm_w=norm_w, final_norm_w=final_norm_w, even_w_in=even_w_in, even_w_out=even_w_out,
             ssm_lambda_re=ssm_lambda_re, ssm_lambda_im=ssm_lambda_im, ssm_log_step=ssm_log_step,
             ssm_b_re=ssm_b_re, ssm_b_im=ssm_b_im, ssm_c_re=ssm_c_re, ssm_c_im=ssm_c_im, ssm_d=ssm_d,
             ssm_glu_w=ssm_glu_w, ssm_glu_b=ssm_glu_b, hgrn_lower_bounds=hgrn_lower_bounds,
             hgrn_norm_w=hgrn_norm_w, rw_mix=rw_mix, rw_w_rkvz=rw_w_rkvz, rw_w0=rw_w0, rw_w1=rw_w1,
             rw_w2=rw_w2, rw_a0=rw_a0, rw_a1=rw_a1, rw_a2=rw_a2, rw_v0=rw_v0, rw_v1=rw_v1, rw_v2=rw_v2,
             rw_g1=rw_g1, rw_g2=rw_g2, rw_k_k=rw_k_k, rw_k_a=rw_k_a, rw_r_k=rw_r_k,
             rw_ln_w=rw_ln_w, rw_ln_b=rw_ln_b, rw_w_o=rw_w_o)
    for name in MATMUL_WEIGHTS:
        w[name] = w[name].astype(BF16)
    n_even, n_odd = state_hgrn.shape[0], state_wkv.shape[0]
    bp, bs, ts = x_prompt.shape[0], x_sample.shape[0], x_sample.shape[1]
    assert ts == SAMPLE_CFG["t_valid"]
    z_re = jnp.zeros((n_even, bp, A_GROUPS, A_STATE), F32)
    z_hg = jnp.zeros((n_even, bp, B_HEADS, B_DIM, B_DIM), F32)
    z_wkv = jnp.zeros((n_odd, bp, C_HEADS, C_HEAD, C_HEAD), F32)
    z_sh = jnp.zeros((n_odd, bp, D_MODEL), F32)
    prompt = _trunk(x_prompt, z_re, z_re, z_hg, z_wkv, z_sh, w, PROMPT_CFG)
    xs = jnp.pad(x_sample, ((0, 0), (0, SAMPLE_PAD - ts), (0, 0)))
    sample = _trunk(xs, state_ssm_re, state_ssm_im, state_hgrn, state_wkv, state_shift, w, SAMPLE_CFG)
    y_sample = sample[0][:, :ts]
    return (prompt[0], y_sample, *prompt[1:], *sample[1:])
```

```python
import functools
import math

import jax
import jax.numpy as jnp
from jax import lax
from jax.experimental import pallas as pl
from jax.experimental.pallas import tpu as pltpu

F32 = jnp.float32
BF16 = jnp.bfloat16

D_MODEL = 1024
A_WIDTH = 512
A_GROUP = 16
A_GROUPS = 32
A_STATE = 64
A_LANES = A_GROUPS * A_STATE
A_BLOCKS = 4
B_WIDTH = 512
B_HEADS = 4
B_DIM = 128
C_HEADS = 16
C_HEAD = 64
C_PACK = 4
DECAY_SCALE = math.exp(-0.5)
RMS_EPS = 1e-6
GN_EPS = 64e-5
VMEM_LIMIT = 56 * 1024 * 1024
LANES = 128


def _dot(a, b):
    return jnp.dot(a.astype(b.dtype), b, preferred_element_type=F32)


def _bdot(a, b, contract):
    return lax.dot_general(a, b, (contract, ((0,), (0,))), preferred_element_type=F32)


def _nn(a, b):
    return _bdot(a, b, ((2,), (1,)))


def _nt(a, b):
    return _bdot(a, b, ((2,), (2,)))


def _tn(a, b):
    return _bdot(a, b, ((1,), (1,)))


def _rms(x, w):
    return x * lax.rsqrt(jnp.mean(x * x, axis=-1, keepdims=True) + RMS_EPS) * w


def _sigmoid(x):
    return 1.0 / (1.0 + jnp.exp(-x))


def _silu(x):
    return x * _sigmoid(x)


def _params(sem):
    return pltpu.CompilerParams(dimension_semantics=sem, vmem_limit_bytes=VMEM_LIMIT)


def _stacked_state_specs(prev, state_shape, n_layers, layer, bb):
    tail = (0,) * (len(state_shape) - 1)
    shape = jax.ShapeDtypeStruct((n_layers, *state_shape), F32)
    block = (bb, *state_shape[1:])
    if prev is None:
        return pl.BlockSpec((n_layers, *block), lambda i, j: (0, i, *tail)), shape, layer, False
    return pl.BlockSpec((1, *block), lambda i, j: (layer, i, *tail)), shape, 0, True


def _write_state_slab(ref, slot, aliased, value):
    if not aliased:
        for other in range(ref.shape[0]):
            if other != slot:
                ref[other] = jnp.zeros(ref.shape[1:], F32)
    ref[slot] = value


def _tri_masks(c):
    row = lax.broadcasted_iota(jnp.int32, (c, c), 0)
    col = lax.broadcasted_iota(jnp.int32, (c, c), 1)
    return row >= col, row > col, row == col


def _split_heads(x, width):
    bb, _, d = x.shape
    return jnp.stack([x[i, :, h * width:(h + 1) * width] for i in range(bb) for h in range(d // width)], axis=0)


def _merge_heads(x, bb):
    h = x.shape[0] // bb
    return jnp.stack([jnp.concatenate([x[i * h + j] for j in range(h)], axis=-1) for i in range(bb)], axis=0)


def _cumsum_time(tri_f, x):
    hi = x.astype(BF16).astype(F32)
    rest = x - hi
    mid = rest.astype(BF16).astype(F32)
    lo = rest - mid
    return jnp.stack([sum(jnp.dot(tri_f, p[i], preferred_element_type=F32) for p in (hi, mid, lo))
                      for i in range(x.shape[0])], axis=0)


def _norm_matmul_kernel(x_ref, nw_ref, w_ref, o_ref):
    o_ref[...] = _dot(_rms(x_ref[...], nw_ref[...]), w_ref[...])


def _norm_matmul(x, nw, w, *, tm, tn):
    m, k = x.shape
    n = w.shape[1]
    return pl.pallas_call(
        _norm_matmul_kernel, grid=(m // tm, n // tn),
        in_specs=[pl.BlockSpec((tm, k), lambda i, j: (i, 0)),
                  pl.BlockSpec((1, k), lambda i, j: (0, 0)),
                  pl.BlockSpec((k, tn), lambda i, j: (0, j))],
        out_specs=pl.BlockSpec((tm, tn), lambda i, j: (i, j)),
        out_shape=jax.ShapeDtypeStruct((m, n), F32),
        compiler_params=_params(("parallel", "arbitrary")), name="norm_in_proj",
    )(x, nw, w)


def _proj_residual_kernel(*refs, n_x, final_norm):
    h_ref, x_refs, w_ref = refs[0], refs[1:1 + n_x], refs[1 + n_x]
    x = jnp.concatenate([r[...] for r in x_refs], axis=-1) if n_x > 1 else x_refs[0][...]
    h = h_ref[...] + _dot(x, w_ref[...])
    if final_norm:
        nw_ref, h_out, y_out = refs[2 + n_x:]
        y_out[...] = _rms(h, nw_ref[...])
    else:
        h_out = refs[2 + n_x]
    h_out[...] = h


def _proj_residual(h, xs, w, final_nw=None, *, tm):
    m, n = h.shape
    final_norm = final_nw is not None
    row = lambda width: pl.BlockSpec((tm, width), lambda i: (i, 0))
    in_specs = [row(n)] + [row(x.shape[1]) for x in xs] + [pl.BlockSpec(w.shape, lambda i: (0, 0))]
    args = [h, *xs, w]
    out_specs, out_shape = [row(n)], [jax.ShapeDtypeStruct((m, n), F32)]
    if final_norm:
        in_specs.append(pl.BlockSpec((1, n), lambda i: (0, 0)))
        args.append(final_nw)
        out_specs.append(row(n))
        out_shape.append(jax.ShapeDtypeStruct((m, n), F32))
    return pl.pallas_call(
        functools.partial(_proj_residual_kernel, n_x=len(xs), final_norm=final_norm), grid=(m // tm,),
        in_specs=in_specs, out_specs=out_specs, out_shape=out_shape,
        compiler_params=_params(("parallel",)), name="out_proj_residual",
    )(*args)


def _s5_kernel(u_ref, za_ref, h0r_ref, h0i_ref, ar_ref, ai_ref, bbr_ref, bbi_ref, cr_ref, ci_ref,
               d_ref, gw_ref, gb_ref, out_ref, hfr_ref, hfi_ref, hre, him, bur, bui, st_r, st_i, wbm, wtm,
               *, bsz, tc, n_steps):
    i = pl.program_id(1)
    rows = bsz * tc
    n_tiles = A_LANES // LANES
    u_tiles = A_WIDTH // LANES
    per = n_tiles // A_BLOCKS
    assert A_WIDTH // A_BLOCKS == LANES

    @pl.when(i == 0)
    def _():
        for cb in range(n_tiles):
            st_r[cb] = h0r_ref[:, cb * LANES:(cb + 1) * LANES]
            st_i[cb] = h0i_ref[:, cb * LANES:(cb + 1) * LANES]

    def bm(t):
        return pl.ds(t, bsz, stride=tc)

    def tm(t):
        return pl.ds(pl.multiple_of(t * bsz, bsz), bsz)

    def reorder(src, dst, src_rows, dst_rows):
        def body(t, carry):
            for q in range(u_tiles):
                dst.at[q][dst_rows(t), :] = src.at[q][src_rows(t), :]
            return carry
        lax.fori_loop(0, tc, body, 0)

    u_bm = u_ref[...].reshape(rows, A_WIDTH)
    for q in range(u_tiles):
        wbm[q] = u_bm[:, q * LANES:(q + 1) * LANES]
    reorder(wbm, wtm, bm, tm)

    for j in range(A_BLOCKS):
        bu_r = _dot(wtm[j], bbr_ref[j])
        bu_i = _dot(wtm[j], bbi_ref[j])
        for q in range(per):
            bur[j * per + q] = bu_r[:, q * LANES:(q + 1) * LANES]
            bui[j * per + q] = bu_i[:, q * LANES:(q + 1) * LANES]

    def advance(t, prev):
        new = []
        for cb in range(n_tiles):
            lanes = slice(cb * LANES, (cb + 1) * LANES)
            ar, ai = ar_ref[:, lanes], ai_ref[:, lanes]
            pr, pi = prev[2 * cb], prev[2 * cb + 1]
            hr = ar * pr - ai * pi + bur[cb, tm(t), :]
            hi = ar * pi + ai * pr + bui[cb, tm(t), :]
            hre[cb, tm(t), :] = hr
            him[cb, tm(t), :] = hi
            new += [hr, hi]
        return tuple(new)

    state = tuple(ref[cb] for cb in range(n_tiles) for ref in (st_r, st_i))
    if n_steps <= 8:
        for t in range(n_steps):
            state = advance(t, state)
    else:
        state = lax.fori_loop(0, n_steps, advance, state)
    for cb in range(n_tiles):
        st_r[cb] = state[2 * cb]
        st_i[cb] = state[2 * cb + 1]
    if n_steps < tc:
        pad = pl.ds(n_steps * bsz, (tc - n_steps) * bsz)
        for cb in range(n_tiles):
            hre[cb, pad, :] = bur[cb, pad, :]
            him[cb, pad, :] = bui[cb, pad, :]

    def block(ref, j):
        return jnp.concatenate([ref[j * per + q] for q in range(per)], axis=-1)

    u_tm = jnp.concatenate([wtm[q] for q in range(u_tiles)], axis=-1)
    y = jnp.concatenate(
        [_dot(block(hre, j), cr_ref[j]) - _dot(block(him, j), ci_ref[j]) for j in range(A_BLOCKS)],
        axis=-1) + d_ref[...] * u_tm
    y = 0.5 * y * (1.0 + jnp.tanh(math.sqrt(2.0 / math.pi) * (y + 0.044715 * (y * y * y))))
    y = y * _sigmoid(_dot(y, gw_ref[...]) + gb_ref[...])
    for q in range(u_tiles):
        wtm[q] = y[:, q * LANES:(q + 1) * LANES]
    reorder(wtm, wbm, tm, bm)
    y_bm = jnp.concatenate([wbm[q] for q in range(u_tiles)], axis=-1)
    out = y_bm * _silu(za_ref[...].reshape(rows, A_WIDTH))
    out_ref[...] = out.reshape(bsz, tc, A_WIDTH)

    @pl.when(i == pl.num_programs(1) - 1)
    def _():
        for cb in range(n_tiles):
            hfr_ref[:, cb * LANES:(cb + 1) * LANES] = st_r[cb]
            hfi_ref[:, cb * LANES:(cb + 1) * LANES] = st_i[cb]


def _s5_mixer(proj, h0r, h0i, sp, *, bb, tc, t_valid):
    bsz, t, _ = proj.shape
    n_steps = tc if t_valid is None else t_valid
    full = lambda a: pl.BlockSpec(a.shape, lambda b, i: (0,) * a.ndim)
    col = lambda c: pl.BlockSpec((bb, tc, A_WIDTH), lambda b, i: (b, i, c))
    st = pl.BlockSpec((bb, A_LANES), lambda b, i: (b, 0))
    weights = [sp["ab_re"], sp["ab_im"], sp["bb_re"], sp["bb_im"], sp["c_re"], sp["c_im"],
               sp["d"], sp["glu_w"], sp["glu_b"]]
    state = jax.ShapeDtypeStruct((bsz, A_LANES), F32)
    return pl.pallas_call(
        functools.partial(_s5_kernel, bsz=bb, tc=tc, n_steps=n_steps), grid=(bsz // bb, t // tc),
        in_specs=[col(0), col(1), st, st] + [full(a) for a in weights],
        out_specs=[col(0), st, st],
        out_shape=[jax.ShapeDtypeStruct((bsz, t, A_WIDTH), F32), state, state],
        scratch_shapes=[pltpu.VMEM((A_LANES // LANES, bb * tc, LANES), F32)] * 4
        + [pltpu.VMEM((A_LANES // LANES, bb, LANES), F32)] * 2
        + [pltpu.VMEM((A_WIDTH // LANES, bb * tc, LANES), F32)] * 2,
        compiler_params=_params(("parallel", "arbitrary")), name="s5_mixer",
    )(proj, proj, h0r, h0i, *weights)


def _hgrn_kernel(*refs, bb, chunk, n_chunks, t_valid, use_lb, slot, aliased):
    q_ref, f_ref, i_ref, zb_ref, lb_ref, nw_ref, s0_ref = refs[:7]
    out_ref, sfin_ref, s_scr = refs[-3:]
    c = chunk
    tb = pl.program_id(1)

    @pl.when(tb == 0)
    def _():
        s_scr[...] = jnp.swapaxes(s0_ref[0].reshape(bb * B_HEADS, B_DIM, B_DIM), 1, 2)

    incl, _, _ = _tri_masks(c)
    tri_f = incl.astype(F32)

    def body(ci, carry):
        off = pl.multiple_of(ci * c, c)
        sl = pl.ds(off, c)
        q = _silu(q_ref[:, sl, :])
        f = f_ref[:, sl, :]
        if use_lb:
            lb = lb_ref[...]
            log_f = jnp.log(lb + (1.0 - lb) * _sigmoid(f))
            k = (1.0 - lb) * _sigmoid(-f)
        else:
            log_f = jnp.minimum(f, 0.0) - jnp.log1p(jnp.exp(-jnp.abs(f)))
            k = _sigmoid(-f)
        if t_valid is not None:
            tok = tb * (n_chunks * c) + off + lax.broadcasted_iota(jnp.int32, (1, c, 1), 1)
            log_f = jnp.where(tok < t_valid, log_f, 0.0)
            k = jnp.where(tok < t_valid, k, 0.0)
        cum = _cumsum_time(tri_f, log_f)
        last = cum[:, c - 1:c, :]
        qh = _split_heads(q * jnp.exp(cum), B_DIM)
        kh = _split_heads(k * jnp.exp(-cum), B_DIM)
        ke = _split_heads(k * jnp.exp(last - cum), B_DIM)
        vh = _split_heads(i_ref[:, sl, :], B_DIM)
        gl = _split_heads(jnp.exp(last), B_DIM)
        s = s_scr[...]
        att = jnp.where(incl[None], _nt(qh, kh), 0.0)
        o = _nn(att, vh) + _nt(qh, s)
        s_scr[...] = s * gl + _tn(vh, ke)
        o = o * lax.rsqrt(jnp.mean(o * o, axis=-1, keepdims=True) + RMS_EPS) * nw_ref[...]
        out_ref[:, sl, :] = _merge_heads(o, bb) * _silu(zb_ref[:, sl, :])
        return carry

    lax.fori_loop(0, n_chunks, body, 0)

    @pl.when(tb == pl.num_programs(1) - 1)
    def _():
        _write_state_slab(sfin_ref, slot, aliased,
                          jnp.swapaxes(s_scr[...], 1, 2).reshape(bb, B_HEADS, B_DIM, B_DIM))


def _hgrn_mixer(proj, s0, lb, nw, prev, *, layer, n_layers, bb, tblock, chunk, t_valid, use_lb):
    bsz, t, _ = proj.shape
    st_out, st_shape, slot, aliased = _stacked_state_specs(prev, s0.shape[1:], n_layers, layer, bb)
    extra = ([pl.BlockSpec(memory_space=pl.ANY)], [prev], {7: 1}) if aliased else ([], [], {})
    col = lambda c: pl.BlockSpec((bb, tblock, B_WIDTH), lambda i, j: (i, j, c))
    vec = lambda a: pl.BlockSpec(a.shape, lambda i, j: (0, 0))
    st = pl.BlockSpec((1, bb, B_HEADS, B_DIM, B_DIM), lambda i, j: (layer, i, 0, 0, 0))
    return pl.pallas_call(
        functools.partial(_hgrn_kernel, bb=bb, chunk=chunk, n_chunks=tblock // chunk, t_valid=t_valid,
                          use_lb=use_lb, slot=slot, aliased=aliased),
        grid=(bsz // bb, t // tblock),
        in_specs=[col(2), col(3), col(4), col(5), vec(lb), vec(nw), st] + extra[0],
        out_specs=[pl.BlockSpec((bb, tblock, B_WIDTH), lambda i, j: (i, j, 0)), st_out],
        out_shape=[jax.ShapeDtypeStruct((bsz, t, B_WIDTH), F32), st_shape],
        scratch_shapes=[pltpu.VMEM((bb * B_HEADS, B_DIM, B_DIM), F32)],
        input_output_aliases=extra[2],
        compiler_params=_params(("parallel", "arbitrary")), name="hgrn2_mixer",
    )(proj, proj, proj, proj, lb, nw, s0, *extra[1])


def _rwkv_in_kernel(*refs, tblock, last_row, vres):
    h_ref, nw_ref, sh_ref, mix_ref, wp_ref, w0, w1, w2, a0, a1, a2, g1, g2 = refs[:13]
    if vres:
        v0, v1, v2, vf_ref = refs[13:17]
    r_ref, k_ref, v_ref, gate_ref, lw_ref, al_ref, last_ref, carry = refs[-8:]

    @pl.when(pl.program_id(1) == 0)
    def _():
        carry[...] = sh_ref[...]

    xn = _rms(h_ref[...], nw_ref[...])
    row = lax.broadcasted_iota(jnp.int32, (1, tblock, 1), 1)
    xx = jnp.where(row == 0, carry[...], pltpu.roll(xn, 1, axis=1)) - xn
    carry[...] = xn[:, tblock - 1:tblock, :]
    last_ref[...] = xn[:, last_row:last_row + 1, :]
    shape = xn.shape
    xn = xn.reshape(-1, shape[-1])
    xx = xx.reshape(-1, shape[-1])
    mixed = lambda m: xn + xx * mix_ref[m:m + 1, :]
    store = lambda ref, val: ref.__setitem__(Ellipsis, val.reshape(shape).astype(ref.dtype))
    x_v, x_g = mixed(3), mixed(5)
    store(r_ref, _dot(mixed(0), wp_ref[0]))
    store(k_ref, _dot(mixed(2), wp_ref[1]))
    v = _dot(x_v, wp_ref[2])
    if vres:
        v_first = vf_ref[...].reshape(v.shape).astype(F32)
        v = v + (v_first - v) * _sigmoid(v0[...] + _dot(_dot(x_v, v1[...]), v2[...]))
    store(v_ref, v)
    store(gate_ref, _dot(_sigmoid(_dot(x_g, g1[...])), g2[...]) * _silu(_dot(x_g, wp_ref[3])))
    store(lw_ref, -DECAY_SCALE * _sigmoid(w0[...] + _dot(jnp.tanh(_dot(mixed(1), w1[...])), w2[...])))
    store(al_ref, _sigmoid(a0[...] + _dot(_dot(mixed(4), a1[...]), a2[...])))


def _rwkv_in(h, nw, shift, mix, wp, lora_w, v_first, *, bb, tblock, t_valid, act_dtype):
    bsz, t, d = h.shape
    vres = v_first is not None
    tok = pl.BlockSpec((bb, tblock, d), lambda i, j: (i, j, 0))
    one = pl.BlockSpec((bb, 1, d), lambda i, j: (i, 0, 0))
    full = lambda a: pl.BlockSpec(a.shape, lambda i, j: (0,) * a.ndim)
    in_specs = [tok, full(nw), one, full(mix), full(wp)] + [full(a) for a in lora_w]
    args = [h, nw, shift.reshape(bsz, 1, d), mix, wp, *lora_w]
    if vres:
        in_specs.append(tok)
        args.append(v_first)
    act = lambda dt: jax.ShapeDtypeStruct((bsz, t, d), dt)
    last_row = (tblock if t_valid is None else t_valid) - 1
    return pl.pallas_call(
        functools.partial(_rwkv_in_kernel, tblock=tblock, last_row=last_row, vres=vres),
        grid=(bsz // bb, t // tblock),
        in_specs=in_specs, out_specs=[tok] * 6 + [one],
        out_shape=[act(act_dtype)] * 4 + [act(F32)] * 2 + [jax.ShapeDtypeStruct((bsz, 1, d), F32)],
        scratch_shapes=[pltpu.VMEM((bb, 1, d), F32)],
        compiler_params=_params(("parallel", "arbitrary")), name="rwkv_in_proj",
    )(*args)


def _rwkv_kernel(*refs, bb, chunk, n_chunks, t_valid, slot, aliased):
    r_ref, k_ref, v_ref, gate_ref, lw_ref, al_ref, kk_ref, ka_ref, rk_ref, lnw_ref, lnb_ref, seg_ref, s0_ref = refs[:13]
    y_ref, sfin_ref, s_scr = refs[-3:]
    c = chunk
    gw = C_PACK * C_HEAD
    aw = C_PACK * c
    n_groups = D_MODEL // gw
    n_sq = int(math.log2(c)) - 1
    tb = pl.program_id(1)

    @pl.when(tb == 0)
    def _():
        for b in range(bb):
            s_scr[b] = jnp.concatenate([s0_ref[0, b, h] for h in range(C_HEADS)], axis=-1)

    row = lax.broadcasted_iota(jnp.int32, (c, aw), 0)
    col = lax.broadcasted_iota(jnp.int32, (c, aw), 1) % c
    incl, strict = row >= col, row > col
    eye = (row == col).astype(F32)
    tri_f = _tri_masks(c)[0].astype(F32)

    def bd_mask(rows_per_head, lanes, lanes_per_head):
        r = lax.broadcasted_iota(jnp.int32, (C_PACK * rows_per_head, lanes), 0) // rows_per_head
        l = lax.broadcasted_iota(jnp.int32, (C_PACK * rows_per_head, lanes), 1) // lanes_per_head
        return r == l

    bd_c, bd_v, bd_p = bd_mask(c, gw, C_HEAD), bd_mask(C_HEAD, gw, C_HEAD), bd_mask(c, aw, c)
    lane_head = lax.broadcasted_iota(jnp.int32, (1, gw), 1) // C_HEAD
    seg = seg_ref[...]

    def bd(y, mask):
        return jnp.where(mask, jnp.tile(y, (C_PACK, 1)), 0.0)

    def seg_sum(x, pieces=1):
        total = None
        for _ in range(pieces):
            hi = x.astype(BF16).astype(F32)
            part = jnp.dot(hi, seg, preferred_element_type=F32)
            total = part if total is None else total + part
            x = x - hi
        return total

    def nn(a, b):
        return jnp.dot(a, b, preferred_element_type=F32)

    def nt(a, b):
        return lax.dot_general(a, b, (((1,), (1,)), ((), ())), preferred_element_type=F32)

    def tn(a, b):
        return lax.dot_general(a, b, (((0,), (0,)), ((), ())), preferred_element_type=F32)

    def cat(a, b):
        return jnp.concatenate([a, b], axis=0)

    def body(ci, carry):
        off = pl.multiple_of(ci * c, c)
        sl = pl.ds(off, c)
        inst = [(b, slice(g * gw, (g + 1) * gw)) for b in range(bb) for g in range(n_groups)]
        r_l, k_l, v_l, kk_l, al_l, rk_l, gi_l, gv_l, gm_l = ([] for _ in range(9))
        for b in range(bb):
            r = r_ref[b, sl, :].astype(F32)
            k = k_ref[b, sl, :].astype(F32)
            v = v_ref[b, sl, :].astype(F32)
            lw = lw_ref[b, sl, :]
            al = al_ref[b, sl, :]
            kk_raw = k * kk_ref[...]
            k = k * (1.0 + (al - 1.0) * ka_ref[...])
            rk = r * k * rk_ref[...]
            if t_valid is not None:
                ok = tb * (n_chunks * c) + off + lax.broadcasted_iota(jnp.int32, (c, 1), 0) < t_valid
                lw = jnp.where(ok, lw, 0.0)
                k = jnp.where(ok, k, 0.0)
                al = jnp.where(ok, al, 0.0)
            cum = _cumsum_time(tri_f, lw[None])[0]
            g_in, g_inv, g_m = jnp.exp(cum), jnp.exp(-cum), jnp.exp(cum - lw)
            for g in range(n_groups):
                ln = slice(g * gw, (g + 1) * gw)
                for lst, val in ((r_l, r), (k_l, k), (v_l, v), (kk_l, kk_raw), (al_l, al), (rk_l, rk),
                                 (gi_l, g_in), (gv_l, g_inv), (gm_l, g_m)):
                    lst.append(val[:, ln])
        n = len(inst)
        each = range(n)
        ss_l = [seg_sum(kk_l[i] * kk_l[i], pieces=2) for i in each]
        kk_l = [kk_l[i] / jnp.maximum(jnp.sqrt(ss_l[i]), 1e-12) for i in each]
        kt_l = [k_l[i] * gv_l[i] for i in each]
        bt_l = [kk_l[i] * al_l[i] * gv_l[i] for i in each]
        x_l = [cat(-kk_l[i] * gm_l[i], r_l[i] * gi_l[i]) for i in each]
        s_l = [s_scr[b, :, ln] for b, ln in inst]
        a_b = [nt(x_l[i], bd(bt_l[i], bd_c)) for i in each]
        a_k = [nt(x_l[i], bd(kt_l[i], bd_c)) for i in each]
        x_s = [nt(x_l[i], bd(s_l[i], bd_v)) for i in each]
        n_ab = [jnp.where(strict, a_b[i][:c], 0.0) for i in each]
        a_rb = [jnp.where(incl, a_b[i][c:], 0.0) for i in each]
        a_kk = [cat(jnp.where(strict, a_k[i][:c], 0.0), jnp.where(incl, a_k[i][c:], 0.0)) for i in each]
        a_v = [nn(a_kk[i], bd(v_l[i], bd_c)) for i in each]
        w_m = [x_s[i][:c] + a_v[i][:c] for i in each]
        t_m = [eye + n_ab[i] for i in each]
        p = n_ab
        if n_sq >= 1:
            p = [nn(p[i], bd(p[i], bd_p)) for i in each]
            for _ in range(n_sq - 1):
                both = [nn(cat(p[i], t_m[i]), bd(p[i], bd_p)) for i in each]
                p = [both[i][:c] for i in each]
                t_m = [t_m[i] + both[i][c:] for i in each]
            t_m = [t_m[i] + nn(t_m[i], bd(p[i], bd_p)) for i in each]
        u = [nn(t_m[i], bd(w_m[i], bd_c)) for i in each]
        y = [x_s[i][c:] + a_v[i][c:] for i in each]
        y_u = [nn(a_rb[i], bd(u[i], bd_c)) for i in each]
        full = [tn(cat(v_l[i], u[i]), cat(kt_l[i], bt_l[i])) for i in each]
        for i, (b, ln) in enumerate(inst):
            upd = sum(jnp.where(lane_head == h, full[i][h * C_HEAD:(h + 1) * C_HEAD], 0.0) for h in range(C_PACK))
            s_scr[b, :, ln] = (s_l[i] + upd) * gi_l[i][c - 1:c]
        y = [y[i] + y_u[i] for i in each]
        mu = [seg_sum(y[i]) * (1.0 / C_HEAD) for i in each]
        dev = [y[i] - mu[i] for i in each]
        var = [seg_sum(dev[i] * dev[i]) * (1.0 / C_HEAD) for i in each]
        bonus = [seg_sum(rk_l[i]) * v_l[i] for i in each]
        yn = [dev[i] * lax.rsqrt(var[i] + GN_EPS) for i in each]
        for b in range(bb):
            mine = range(b * n_groups, (b + 1) * n_groups)
            yn_b = jnp.concatenate([yn[i] for i in mine], axis=-1)
            bonus_b = jnp.concatenate([bonus[i] for i in mine], axis=-1)
            y_ref[b, sl, :] = (yn_b * lnw_ref[...] + lnb_ref[...] + bonus_b) * gate_ref[b, sl, :].astype(F32)
        return carry

    lax.fori_loop(0, n_chunks, body, 0)

    @pl.when(tb == pl.num_programs(1) - 1)
    def _():
        final = jnp.stack([jnp.stack([s_scr[b, :, h * C_HEAD:(h + 1) * C_HEAD] for h in range(C_HEADS)])
                           for b in range(bb)])
        _write_state_slab(sfin_ref, slot, aliased, final)


def _rwkv_mixer(r, k, v, gate, lw, al, vecs, s0, prev, *, layer, n_layers, bb, tblock, chunk, t_valid):
    bsz, t, d = r.shape
    st_out, st_shape, slot, aliased = _stacked_state_specs(prev, s0.shape[1:], n_layers, layer, bb)
    tok = pl.BlockSpec((bb, tblock, d), lambda i, j: (i, j, 0))
    vec = pl.BlockSpec((1, d), lambda i, j: (0, 0))
    st = pl.BlockSpec((1, bb, C_HEADS, C_HEAD, C_HEAD), lambda i, j: (layer, i, 0, 0, 0))
    seg = jnp.kron(jnp.eye(C_PACK, dtype=F32), jnp.ones((C_HEAD, C_HEAD), F32))
    in_specs = [tok] * 6 + [vec] * 5 + [pl.BlockSpec(seg.shape, lambda i, j: (0, 0)), st]
    args = [r, k, v, gate, lw, al, *vecs, seg, s0]
    aliases = {}
    if aliased:
        aliases = {len(args): 1}
        in_specs.append(pl.BlockSpec(memory_space=pl.ANY))
        args.append(prev)
    return pl.pallas_call(
        functools.partial(_rwkv_kernel, bb=bb, chunk=chunk, n_chunks=tblock // chunk, t_valid=t_valid,
                          slot=slot, aliased=aliased),
        grid=(bsz // bb, t // tblock),
        in_specs=in_specs, out_specs=[tok, st_out], out_shape=[jax.ShapeDtypeStruct((bsz, t, d), F32), st_shape],
        input_output_aliases=aliases,
        scratch_shapes=[pltpu.VMEM((bb, C_HEAD, d), F32)],
        compiler_params=_params(("parallel", "arbitrary")), name="rwkv7_mixer",
    )(*args)


def _s5_params(w, j):
    lr = jnp.minimum(w["ssm_lambda_re"][j], -1e-4)
    li = w["ssm_lambda_im"][j]
    step = jnp.exp(w["ssm_log_step"][j])[:, None]
    mag = jnp.exp(lr * step)
    ab_re = mag * jnp.cos(li * step)
    ab_im = mag * jnp.sin(li * step)
    den = lr * lr + li * li
    nr = ab_re - 1.0
    cr = (nr * lr + ab_im * li) / den
    ci = (ab_im * lr - nr * li) / den
    b_re, b_im = w["ssm_b_re"][j], w["ssm_b_im"][j]
    bb_re = cr[..., None] * b_re - ci[..., None] * b_im
    bb_im = cr[..., None] * b_im + ci[..., None] * b_re
    gpb = A_GROUPS // A_BLOCKS
    eye = jnp.eye(gpb, dtype=F32)

    def pack_in(bb):
        bb = bb.reshape(A_BLOCKS, gpb, A_STATE, A_GROUP)
        return jnp.einsum("bgph,gk->bghkp", bb, eye).reshape(A_BLOCKS, gpb * A_GROUP, gpb * A_STATE)

    def pack_out(c):
        c = c.reshape(A_BLOCKS, gpb, A_GROUP, A_STATE)
        return jnp.einsum("bghp,gk->bgpkh", c, eye).reshape(A_BLOCKS, gpb * A_STATE, gpb * A_GROUP)

    return dict(ab_re=ab_re.reshape(1, A_LANES), ab_im=ab_im.reshape(1, A_LANES),
                bb_re=pack_in(bb_re).astype(BF16), bb_im=pack_in(bb_im).astype(BF16),
                c_re=pack_out(w["ssm_c_re"][j]).astype(BF16), c_im=pack_out(w["ssm_c_im"][j]).astype(BF16),
                d=w["ssm_d"][j].reshape(1, A_WIDTH), glu_w=w["ssm_glu_w"][j],
                glu_b=w["ssm_glu_b"][j].reshape(1, A_WIDTH))


def _trunk(x, ssm_re, ssm_im, hgrn, wkv, shift, w, cfg):
    bsz, t, d = x.shape
    t_valid = cfg["t_valid"]
    m = bsz * t
    h = x.reshape(m, d)
    n_re, n_im, n_sh = [], [], []
    hg_all = wkv_all = None
    lbs = jax.nn.softmax(w["hgrn_lower_bounds"], axis=0)
    lbs = jnp.cumsum(lbs, axis=0) - lbs[0]
    v_first = None
    y = None
    depth = w["norm_w"].shape[0]
    for layer in range(depth):
        j = layer // 2
        nw = w["norm_w"][layer].reshape(1, d)
        last = layer == depth - 1
        final_nw = w["final_norm_w"].reshape(1, d) if last else None
        if layer % 2 == 0:
            proj = _norm_matmul(h, nw, w["even_w_in"][j], tm=cfg["tm"], tn=1024).reshape(bsz, t, -1)
            out_a, hr, hi = _s5_mixer(proj, ssm_re[j].reshape(bsz, A_LANES), ssm_im[j].reshape(bsz, A_LANES),
                                      _s5_params(w, j), bb=cfg["s5_bb"], tc=cfg["s5_tc"], t_valid=t_valid)
            out_b, hg_all = _hgrn_mixer(proj, hgrn, lbs[j].reshape(1, B_WIDTH),
                                        w["hgrn_norm_w"][j].reshape(1, B_DIM), hg_all, layer=j,
                                        n_layers=hgrn.shape[0], bb=cfg["hgrn_bb"], tblock=cfg["hgrn_tblock"],
                                        chunk=cfg["chunk"], t_valid=t_valid, use_lb=j > 0)
            res = _proj_residual(h, [out_a.reshape(m, A_WIDTH), out_b.reshape(m, B_WIDTH)], w["even_w_out"][j],
                                 final_nw, tm=cfg["tm"])
            n_re.append(hr.reshape(bsz, A_GROUPS, A_STATE))
            n_im.append(hi.reshape(bsz, A_GROUPS, A_STATE))
        else:
            row = lambda a: a.reshape(1, -1)
            lora_w = [row(w["rw_w0"][j]), w["rw_w1"][j], w["rw_w2"][j], row(w["rw_a0"][j]), w["rw_a1"][j],
                      w["rw_a2"][j], w["rw_g1"][j], w["rw_g2"][j]]
            if v_first is not None:
                lora_w += [row(w["rw_v0"][j - 1]), w["rw_v1"][j - 1], w["rw_v2"][j - 1]]
            r, k, v, gate, lw, al, x_last = _rwkv_in(h.reshape(bsz, t, d), nw, shift[j], w["rw_mix"][j],
                                                     w["rw_w_rkvz"][j], lora_w, v_first, bb=cfg["in_bb"],
                                                     tblock=cfg["in_tblock"], t_valid=t_valid,
                                                     act_dtype=cfg["act_dtype"])
            n_sh.append(x_last.reshape(bsz, d))
            if v_first is None:
                v_first = v
            vecs = [row(w[n][j]) for n in ("rw_k_k", "rw_k_a", "rw_r_k", "rw_ln_w", "rw_ln_b")]
            y_mix, wkv_all = _rwkv_mixer(r, k, v, gate, lw, al, vecs, wkv, wkv_all, layer=j,
                                         n_layers=wkv.shape[0], bb=cfg["bb"], tblock=cfg["tblock"],
                                         chunk=cfg["rwkv_chunk"], t_valid=t_valid)
            res = _proj_residual(h, [y_mix.reshape(m, d)], w["rw_w_o"][j], final_nw, tm=cfg["tm"])
        if last:
            h, y = res
        else:
            h = res[0]
    return (y.reshape(bsz, t, d), jnp.stack(n_re), jnp.stack(n_im), hg_all, wkv_all,
            jnp.stack(n_sh))


MATMUL_WEIGHTS = ("even_w_in", "even_w_out", "ssm_glu_w", "rw_w_rkvz", "rw_w1", "rw_w2", "rw_a1", "rw_a2",
                  "rw_v1", "rw_v2", "rw_g1", "rw_g2", "rw_w_o")
PROMPT_CFG = dict(t_valid=None, tm=1024, s5_bb=8, s5_tc=64, bb=2, tblock=128, chunk=16, rwkv_chunk=64,
                  hgrn_bb=8, hgrn_tblock=128, in_bb=1, in_tblock=512, act_dtype=BF16)
SAMPLE_PAD = 8
SAMPLE_CFG = dict(t_valid=4, tm=1024, s5_bb=32, s5_tc=SAMPLE_PAD, bb=8, tblock=SAMPLE_PAD, chunk=SAMPLE_PAD,
                  rwkv_chunk=SAMPLE_PAD, hgrn_bb=8, hgrn_tblock=SAMPLE_PAD, in_bb=16, in_tblock=SAMPLE_PAD,
                  act_dtype=F32)


def kernel(x_prompt, x_sample, state_ssm_re, state_ssm_im, state_hgrn, state_wkv, state_shift, norm_w, final_norm_w, even_w_in, even_w_out, ssm_lambda_re, ssm_lambda_im, ssm_log_step, ssm_b_re, ssm_b_im, ssm_c_re, ssm_c_im, ssm_d, ssm_glu_w, ssm_glu_b, hgrn_lower_bounds, hgrn_norm_w, rw_mix, rw_w_rkvz, rw_w0, rw_w1, rw_w2, rw_a0, rw_a1, rw_a2, rw_v0, rw_v1, rw_v2, rw_g1, rw_g2, rw_k_k, rw_k_a, rw_r_k, rw_ln_w, rw_ln_b, rw_w_o):
    w = dict(norm_w=norm_w, final_norm_w=final_norm_w, even_w_in=even_w_in, even_w_out=even_w_out,
             ssm_lambda_re=ssm_lambda_re, ssm_lambda_im=ssm_lambda_im, ssm_log_step=ssm_log_step,
             ssm_b_re=ssm_b_re, ssm_b_im=ssm_b_im, ssm_c_re=ssm_c_re, ssm_c_im=ssm_c_im, ssm_d=ssm_d,
             ssm_glu_w=ssm_glu_w, ssm_glu_b=ssm_glu_b, hgrn_lower_bounds=hgrn_lower_bounds,
             hgrn_norm_w=hgrn_norm_w, rw_mix=rw_mix, rw_w_rkvz=rw_w_rkvz, rw_w0=rw_w0, rw_w1=rw_w1,
             rw_w2=rw_w2, rw_a0=rw_a0, rw_a1=rw_a1, rw_a2=rw_a2, rw_v0=rw_v0, rw_v1=rw_v1, rw_v2=rw_v2,
             rw_g1=rw_g1, rw_g2=rw_g2, rw_k_k=rw_k_k, rw_k_a=rw_k_a, rw_r_k=rw_r_k,
             rw_ln_w=rw_ln_w, rw_ln_b=rw_ln_b, rw_w_o=rw_w_o)
    for name in MATMUL_WEIGHTS:
        w[name] = w[name].astype(BF16)
    n_even, n_odd = state_hgrn.shape[0], state_wkv.shape[0]
    bp, bs, ts = x_prompt.shape[0], x_sample.shape[0], x_sample.shape[1]
    assert ts == SAMPLE_CFG["t_valid"]
    z_re = jnp.zeros((n_even, bp, A_GROUPS, A_STATE), F32)
    z_hg = jnp.zeros((n_even, bp, B_HEADS, B_DIM, B_DIM), F32)
    z_wkv = jnp.zeros((n_odd, bp, C_HEADS, C_HEAD, C_HEAD), F32)
    z_sh = jnp.zeros((n_odd, bp, D_MODEL), F32)
    prompt = _trunk(x_prompt, z_re, z_re, z_hg, z_wkv, z_sh, w, PROMPT_CFG)
    xs = jnp.pad(x_sample, ((0, 0), (0, SAMPLE_PAD - ts), (0, 0)))
    sample = _trunk(xs, state_ssm_re, state_ssm_im, state_hgrn, state_wkv, state_shift, w, SAMPLE_CFG)
    y_sample = sample[0][:, :ts]
    return (prompt[0], y_sample, *prompt[1:], *sample[1:])
```

```python
import functools
import math

import jax
import jax.numpy as jnp
from jax import lax
from jax.experimental import pallas as pl
from jax.experimental.pallas import tpu as pltpu

F32 = jnp.float32
BF16 = jnp.bfloat16

D_MODEL = 1024
A_WIDTH = 512
A_GROUP = 16
A_GROUPS = 32
A_STATE = 64
A_LANES = A_GROUPS * A_STATE
A_BLOCKS = 4
B_WIDTH = 512
B_HEADS = 4
B_DIM = 128
C_HEADS = 16
C_HEAD = 64
C_PACK = 4
DECAY_SCALE = math.exp(-0.5)
RMS_EPS = 1e-6
GN_EPS = 64e-5
VMEM_LIMIT = 56 * 1024 * 1024
LANES = 128


def _dot(a, b):
    return jnp.dot(a.astype(b.dtype), b, preferred_element_type=F32)


def _bdot(a, b, contract):
    return lax.dot_general(a, b, (contract, ((0,), (0,))), preferred_element_type=F32)


def _nn(a, b):
    return _bdot(a, b, ((2,), (1,)))


def _nt(a, b):
    return _bdot(a, b, ((2,), (2,)))


def _tn(a, b):
    return _bdot(a, b, ((1,), (1,)))


def _rms(x, w):
    return x * lax.rsqrt(jnp.mean(x * x, axis=-1, keepdims=True) + RMS_EPS) * w


def _sigmoid(x):
    return 1.0 / (1.0 + jnp.exp(-x))


def _silu(x):
    return x * _sigmoid(x)


def _params(sem):
    return pltpu.CompilerParams(dimension_semantics=sem, vmem_limit_bytes=VMEM_LIMIT)


def _stacked_state_specs(prev, state_shape, n_layers, layer, bb):
    tail = (0,) * (len(state_shape) - 1)
    shape = jax.ShapeDtypeStruct((n_layers, *state_shape), F32)
    block = (bb, *state_shape[1:])
    if prev is None:
        return pl.BlockSpec((n_layers, *block), lambda i, j: (0, i, *tail)), shape, layer, False
    return pl.BlockSpec((1, *block), lambda i, j: (layer, i, *tail)), shape, 0, True


def _write_state_slab(ref, slot, aliased, value):
    if not aliased:
        for other in range(ref.shape[0]):
            if other != slot:
                ref[other] = jnp.zeros(ref.shape[1:], F32)
    ref[slot] = value


def _tri_masks(c):
    row = lax.broadcasted_iota(jnp.int32, (c, c), 0)
    col = lax.broadcasted_iota(jnp.int32, (c, c), 1)
    return row >= col, row > col, row == col


def _split_heads(x, width):
    bb, _, d = x.shape
    return jnp.stack([x[i, :, h * width:(h + 1) * width] for i in range(bb) for h in range(d // width)], axis=0)


def _merge_heads(x, bb):
    h = x.shape[0] // bb
    return jnp.stack([jnp.concatenate([x[i * h + j] for j in range(h)], axis=-1) for i in range(bb)], axis=0)


def _cumsum_time(tri_f, x):
    hi = x.astype(BF16).astype(F32)
    rest = x - hi
    mid = rest.astype(BF16).astype(F32)
    lo = rest - mid
    return jnp.stack([sum(jnp.dot(tri_f, p[i], preferred_element_type=F32) for p in (hi, mid, lo))
                      for i in range(x.shape[0])], axis=0)


def _norm_matmul_kernel(x_ref, nw_ref, w_ref, o_ref):
    o_ref[...] = _dot(_rms(x_ref[...], nw_ref[...]), w_ref[...])


def _norm_matmul(x, nw, w, *, tm, tn):
    m, k = x.shape
    n = w.shape[1]
    return pl.pallas_call(
        _norm_matmul_kernel, grid=(m // tm, n // tn),
        in_specs=[pl.BlockSpec((tm, k), lambda i, j: (i, 0)),
                  pl.BlockSpec((1, k), lambda i, j: (0, 0)),
                  pl.BlockSpec((k, tn), lambda i, j: (0, j))],
        out_specs=pl.BlockSpec((tm, tn), lambda i, j: (i, j)),
        out_shape=jax.ShapeDtypeStruct((m, n), F32),
        compiler_params=_params(("parallel", "arbitrary")), name="norm_in_proj",
    )(x, nw, w)


def _proj_residual_kernel(*refs, n_x, final_norm):
    h_ref, x_refs, w_ref = refs[0], refs[1:1 + n_x], refs[1 + n_x]
    x = jnp.concatenate([r[...] for r in x_refs], axis=-1) if n_x > 1 else x_refs[0][...]
    h = h_ref[...] + _dot(x, w_ref[...])
    if final_norm:
        nw_ref, h_out, y_out = refs[2 + n_x:]
        y_out[...] = _rms(h, nw_ref[...])
    else:
        h_out = refs[2 + n_x]
    h_out[...] = h


def _proj_residual(h, xs, w, final_nw=None, *, tm):
    m, n = h.shape
    final_norm = final_nw is not None
    row = lambda width: pl.BlockSpec((tm, width), lambda i: (i, 0))
    in_specs = [row(n)] + [row(x.shape[1]) for x in xs] + [pl.BlockSpec(w.shape, lambda i: (0, 0))]
    args = [h, *xs, w]
    out_specs, out_shape = [row(n)], [jax.ShapeDtypeStruct((m, n), F32)]
    if final_norm:
        in_specs.append(pl.BlockSpec((1, n), lambda i: (0, 0)))
        args.append(final_nw)
        out_specs.append(row(n))
        out_shape.append(jax.ShapeDtypeStruct((m, n), F32))
    return pl.pallas_call(
        functools.partial(_proj_residual_kernel, n_x=len(xs), final_norm=final_norm), grid=(m // tm,),
        in_specs=in_specs, out_specs=out_specs, out_shape=out_shape,
        compiler_params=_params(("parallel",)), name="out_proj_residual",
    )(*args)


def _s5_kernel(u_ref, za_ref, h0r_ref, h0i_ref, ar_ref, ai_ref, bbr_ref, bbi_ref, cr_ref, ci_ref,
               d_ref, gw_ref, gb_ref, out_ref, hfr_ref, hfi_ref, hre, him, bur, bui, st_r, st_i, wbm, wtm,
               *, bsz, tc, n_steps):
    i = pl.program_id(1)
    rows = bsz * tc
    n_tiles = A_LANES // LANES
    u_tiles = A_WIDTH // LANES
    per = n_tiles // A_BLOCKS
    assert A_WIDTH // A_BLOCKS == LANES

    @pl.when(i == 0)
    def _():
        for cb in range(n_tiles):
            st_r[cb] = h0r_ref[:, cb * LANES:(cb + 1) * LANES]
            st_i[cb] = h0i_ref[:, cb * LANES:(cb + 1) * LANES]

    def bm(t):
        return pl.ds(t, bsz, stride=tc)

    def tm(t):
        return pl.ds(pl.multiple_of(t * bsz, bsz), bsz)

    def reorder(src, dst, src_rows, dst_rows):
        def body(t, carry):
            for q in range(u_tiles):
                dst.at[q][dst_rows(t), :] = src.at[q][src_rows(t), :]
            return carry
        lax.fori_loop(0, tc, body, 0, unroll=min(tc, 8))

    u_bm = u_ref[...].reshape(rows, A_WIDTH)
    for q in range(u_tiles):
        wbm[q] = u_bm[:, q * LANES:(q + 1) * LANES]
    reorder(wbm, wtm, bm, tm)

    for j in range(A_BLOCKS):
        bu_r = _dot(wtm[j], bbr_ref[j])
        bu_i = _dot(wtm[j], bbi_ref[j])
        for q in range(per):
            bur[j * per + q] = bu_r[:, q * LANES:(q + 1) * LANES]
            bui[j * per + q] = bu_i[:, q * LANES:(q + 1) * LANES]

    def advance(t, prev):
        new = []
        for cb in range(n_tiles):
            lanes = slice(cb * LANES, (cb + 1) * LANES)
            ar, ai = ar_ref[:, lanes], ai_ref[:, lanes]
            pr, pi = prev[2 * cb], prev[2 * cb + 1]
            hr = ar * pr - ai * pi + bur[cb, tm(t), :]
            hi = ar * pi + ai * pr + bui[cb, tm(t), :]
            hre[cb, tm(t), :] = hr
            him[cb, tm(t), :] = hi
            new += [hr, hi]
        return tuple(new)

    state = tuple(ref[cb] for cb in range(n_tiles) for ref in (st_r, st_i))
    if n_steps <= 8:
        for t in range(n_steps):
            state = advance(t, state)
    else:
        state = lax.fori_loop(0, n_steps, advance, state)
    for cb in range(n_tiles):
        st_r[cb] = state[2 * cb]
        st_i[cb] = state[2 * cb + 1]
    if n_steps < tc:
        pad = pl.ds(n_steps * bsz, (tc - n_steps) * bsz)
        for cb in range(n_tiles):
            hre[cb, pad, :] = bur[cb, pad, :]
            him[cb, pad, :] = bui[cb, pad, :]

    def block(ref, j):
        return jnp.concatenate([ref[j * per + q] for q in range(per)], axis=-1)

    u_tm = jnp.concatenate([wtm[q] for q in range(u_tiles)], axis=-1)
    y = jnp.concatenate(
        [_dot(block(hre, j), cr_ref[j]) - _dot(block(him, j), ci_ref[j]) for j in range(A_BLOCKS)],
        axis=-1) + d_ref[...] * u_tm
    y = 0.5 * y * (1.0 + jnp.tanh(math.sqrt(2.0 / math.pi) * (y + 0.044715 * (y * y * y))))
    y = y * _sigmoid(_dot(y, gw_ref[...]) + gb_ref[...])
    for q in range(u_tiles):
        wtm[q] = y[:, q * LANES:(q + 1) * LANES]
    reorder(wtm, wbm, tm, bm)
    y_bm = jnp.concatenate([wbm[q] for q in range(u_tiles)], axis=-1)
    out = y_bm * _silu(za_ref[...].reshape(rows, A_WIDTH))
    out_ref[...] = out.reshape(bsz, tc, A_WIDTH)

    @pl.when(i == pl.num_programs(1) - 1)
    def _():
        for cb in range(n_tiles):
            hfr_ref[:, cb * LANES:(cb + 1) * LANES] = st_r[cb]
            hfi_ref[:, cb * LANES:(cb + 1) * LANES] = st_i[cb]


def _s5_mixer(proj, h0r, h0i, sp, *, bb, tc, t_valid):
    bsz, t, _ = proj.shape
    n_steps = tc if t_valid is None else t_valid
    full = lambda a: pl.BlockSpec(a.shape, lambda b, i: (0,) * a.ndim)
    col = lambda c: pl.BlockSpec((bb, tc, A_WIDTH), lambda b, i: (b, i, c))
    st = pl.BlockSpec((bb, A_LANES), lambda b, i: (b, 0))
    weights = [sp["ab_re"], sp["ab_im"], sp["bb_re"], sp["bb_im"], sp["c_re"], sp["c_im"],
               sp["d"], sp["glu_w"], sp["glu_b"]]
    state = jax.ShapeDtypeStruct((bsz, A_LANES), F32)
    return pl.pallas_call(
        functools.partial(_s5_kernel, bsz=bb, tc=tc, n_steps=n_steps), grid=(bsz // bb, t // tc),
        in_specs=[col(0), col(1), st, st] + [full(a) for a in weights],
        out_specs=[col(0), st, st],
        out_shape=[jax.ShapeDtypeStruct((bsz, t, A_WIDTH), F32), state, state],
        scratch_shapes=[pltpu.VMEM((A_LANES // LANES, bb * tc, LANES), F32)] * 4
        + [pltpu.VMEM((A_LANES // LANES, bb, LANES), F32)] * 2
        + [pltpu.VMEM((A_WIDTH // LANES, bb * tc, LANES), F32)] * 2,
        compiler_params=_params(("parallel", "arbitrary")), name="s5_mixer",
    )(proj, proj, h0r, h0i, *weights)


def _hgrn_kernel(*refs, bb, chunk, n_chunks, t_valid, use_lb, slot, aliased):
    q_ref, f_ref, i_ref, zb_ref, lb_ref, nw_ref, s0_ref = refs[:7]
    out_ref, sfin_ref, s_scr = refs[-3:]
    c = chunk
    tb = pl.program_id(1)

    @pl.when(tb == 0)
    def _():
        s_scr[...] = jnp.swapaxes(s0_ref[0].reshape(bb * B_HEADS, B_DIM, B_DIM), 1, 2)

    incl, _, _ = _tri_masks(c)
    tri_f = incl.astype(F32)

    def body(ci, carry):
        off = pl.multiple_of(ci * c, c)
        sl = pl.ds(off, c)
        q = _silu(q_ref[:, sl, :])
        f = f_ref[:, sl, :]
        if use_lb:
            lb = lb_ref[...]
            log_f = jnp.log(lb + (1.0 - lb) * _sigmoid(f))
            k = (1.0 - lb) * _sigmoid(-f)
        else:
            log_f = jnp.minimum(f, 0.0) - jnp.log1p(jnp.exp(-jnp.abs(f)))
            k = _sigmoid(-f)
        if t_valid is not None:
            tok = tb * (n_chunks * c) + off + lax.broadcasted_iota(jnp.int32, (1, c, 1), 1)
            log_f = jnp.where(tok < t_valid, log_f, 0.0)
            k = jnp.where(tok < t_valid, k, 0.0)
        cum = _cumsum_time(tri_f, log_f)
        last = cum[:, c - 1:c, :]
        mid = cum[:, c // 2 - 1:c // 2, :]
        qh = _split_heads(q * jnp.exp(cum), B_DIM)
        qm = _split_heads(q * jnp.exp(cum - mid), B_DIM)
        kh = _split_heads(k * jnp.exp(mid - cum), B_DIM)
        ke = _split_heads(k * jnp.exp(last - cum), B_DIM)
        vh = _split_heads(i_ref[:, sl, :], B_DIM)
        gl = _split_heads(jnp.exp(last), B_DIM)
        s = s_scr[...]
        att = jnp.where(incl[None], _nt(qm, kh), 0.0)
        o = _nn(att, vh) + _nt(qh, s)
        s_scr[...] = s * gl + _tn(vh, ke)
        o = o * lax.rsqrt(jnp.mean(o * o, axis=-1, keepdims=True) + RMS_EPS) * nw_ref[...]
        out_ref[:, sl, :] = _merge_heads(o, bb) * _silu(zb_ref[:, sl, :])
        return carry

    lax.fori_loop(0, n_chunks, body, 0)

    @pl.when(tb == pl.num_programs(1) - 1)
    def _():
        _write_state_slab(sfin_ref, slot, aliased,
                          jnp.swapaxes(s_scr[...], 1, 2).reshape(bb, B_HEADS, B_DIM, B_DIM))


def _hgrn_mixer(proj, s0, lb, nw, prev, *, layer, n_layers, bb, tblock, chunk, t_valid, use_lb):
    bsz, t, _ = proj.shape
    st_out, st_shape, slot, aliased = _stacked_state_specs(prev, s0.shape[1:], n_layers, layer, bb)
    extra = ([pl.BlockSpec(memory_space=pl.ANY)], [prev], {7: 1}) if aliased else ([], [], {})
    col = lambda c: pl.BlockSpec((bb, tblock, B_WIDTH), lambda i, j: (i, j, c))
    vec = lambda a: pl.BlockSpec(a.shape, lambda i, j: (0, 0))
    st = pl.BlockSpec((1, bb, B_HEADS, B_DIM, B_DIM), lambda i, j: (layer, i, 0, 0, 0))
    return pl.pallas_call(
        functools.partial(_hgrn_kernel, bb=bb, chunk=chunk, n_chunks=tblock // chunk, t_valid=t_valid,
                          use_lb=use_lb, slot=slot, aliased=aliased),
        grid=(bsz // bb, t // tblock),
        in_specs=[col(2), col(3), col(4), col(5), vec(lb), vec(nw), st] + extra[0],
        out_specs=[pl.BlockSpec((bb, tblock, B_WIDTH), lambda i, j: (i, j, 0)), st_out],
        out_shape=[jax.ShapeDtypeStruct((bsz, t, B_WIDTH), F32), st_shape],
        scratch_shapes=[pltpu.VMEM((bb * B_HEADS, B_DIM, B_DIM), F32)],
        input_output_aliases=extra[2],
        compiler_params=_params(("parallel", "arbitrary")), name="hgrn2_mixer",
    )(proj, proj, proj, proj, lb, nw, s0, *extra[1])


def _rwkv_in_kernel(*refs, tblock, last_row, vres):
    h_ref, nw_ref, sh_ref, mix_ref, wp_ref, w0, w1, w2, a0, a1, a2, g1, g2 = refs[:13]
    if vres:
        v0, v1, v2, vf_ref = refs[13:17]
    r_ref, k_ref, v_ref, gate_ref, lw_ref, al_ref, last_ref, carry = refs[-8:]

    @pl.when(pl.program_id(1) == 0)
    def _():
        carry[...] = sh_ref[...]

    xn = _rms(h_ref[...], nw_ref[...])
    row = lax.broadcasted_iota(jnp.int32, (1, tblock, 1), 1)
    xx = jnp.where(row == 0, carry[...], pltpu.roll(xn, 1, axis=1)) - xn
    carry[...] = xn[:, tblock - 1:tblock, :]
    last_ref[...] = xn[:, last_row:last_row + 1, :]
    shape = xn.shape
    xn = xn.reshape(-1, shape[-1])
    xx = xx.reshape(-1, shape[-1])
    mixed = lambda m: xn + xx * mix_ref[m:m + 1, :]
    store = lambda ref, val: ref.__setitem__(Ellipsis, val.reshape(shape).astype(ref.dtype))
    x_v, x_g = mixed(3), mixed(5)
    store(r_ref, _dot(mixed(0), wp_ref[0]))
    store(k_ref, _dot(mixed(2), wp_ref[1]))
    v = _dot(x_v, wp_ref[2])
    if vres:
        v_first = vf_ref[...].reshape(v.shape).astype(F32)
        v = v + (v_first - v) * _sigmoid(v0[...] + _dot(_dot(x_v, v1[...]), v2[...]))
    store(v_ref, v)
    store(gate_ref, _dot(_sigmoid(_dot(x_g, g1[...])), g2[...]) * _silu(_dot(x_g, wp_ref[3])))
    store(lw_ref, -DECAY_SCALE * _sigmoid(w0[...] + _dot(jnp.tanh(_dot(mixed(1), w1[...])), w2[...])))
    store(al_ref, _sigmoid(a0[...] + _dot(_dot(mixed(4), a1[...]), a2[...])))


def _rwkv_in(h, nw, shift, mix, wp, lora_w, v_first, *, bb, tblock, t_valid, act_dtype):
    bsz, t, d = h.shape
    vres = v_first is not None
    tok = pl.BlockSpec((bb, tblock, d), lambda i, j: (i, j, 0))
    one = pl.BlockSpec((bb, 1, d), lambda i, j: (i, 0, 0))
    full = lambda a: pl.BlockSpec(a.shape, lambda i, j: (0,) * a.ndim)
    in_specs = [tok, full(nw), one, full(mix), full(wp)] + [full(a) for a in lora_w]
    args = [h, nw, shift.reshape(bsz, 1, d), mix, wp, *lora_w]
    if vres:
        in_specs.append(tok)
        args.append(v_first)
    act = lambda dt: jax.ShapeDtypeStruct((bsz, t, d), dt)
    last_row = (tblock if t_valid is None else t_valid) - 1
    return pl.pallas_call(
        functools.partial(_rwkv_in_kernel, tblock=tblock, last_row=last_row, vres=vres),
        grid=(bsz // bb, t // tblock),
        in_specs=in_specs, out_specs=[tok] * 6 + [one],
        out_shape=[act(act_dtype)] * 4 + [act(F32)] * 2 + [jax.ShapeDtypeStruct((bsz, 1, d), F32)],
        scratch_shapes=[pltpu.VMEM((bb, 1, d), F32)],
        compiler_params=_params(("parallel", "arbitrary")), name="rwkv_in_proj",
    )(*args)


def _rwkv_kernel(*refs, bb, chunk, n_chunks, t_valid, slot, aliased):
    r_ref, k_ref, v_ref, gate_ref, lw_ref, al_ref, kk_ref, ka_ref, rk_ref, lnw_ref, lnb_ref, seg_ref, s0_ref = refs[:13]
    y_ref, sfin_ref, s_scr = refs[-3:]
    c = chunk
    gw = C_PACK * C_HEAD
    aw = C_PACK * c
    n_groups = D_MODEL // gw
    n_sq = int(math.log2(c)) - 1
    tb = pl.program_id(1)

    @pl.when(tb == 0)
    def _():
        for b in range(bb):
            s_scr[b] = jnp.concatenate([s0_ref[b, h] for h in range(C_HEADS)], axis=-1)

    row = lax.broadcasted_iota(jnp.int32, (c, aw), 0)
    col = lax.broadcasted_iota(jnp.int32, (c, aw), 1) % c
    incl, strict = row >= col, row > col
    eye = (row == col).astype(F32)
    tri_f = _tri_masks(c)[0].astype(F32)

    def bd_mask(rows_per_head, lanes, lanes_per_head):
        r = lax.broadcasted_iota(jnp.int32, (C_PACK * rows_per_head, lanes), 0) // rows_per_head
        l = lax.broadcasted_iota(jnp.int32, (C_PACK * rows_per_head, lanes), 1) // lanes_per_head
        return r == l

    bd_c, bd_v, bd_p = bd_mask(c, gw, C_HEAD), bd_mask(C_HEAD, gw, C_HEAD), bd_mask(c, aw, c)
    lane_head = lax.broadcasted_iota(jnp.int32, (1, gw), 1) // C_HEAD
    seg = seg_ref[...]

    def bd(y, mask):
        return jnp.where(mask, jnp.tile(y, (C_PACK, 1)), 0.0)

    def seg_sum(x, pieces=1):
        total = None
        for _ in range(pieces):
            hi = x.astype(BF16).astype(F32)
            part = jnp.dot(hi, seg, preferred_element_type=F32)
            total = part if total is None else total + part
            x = x - hi
        return total

    def nn(a, b):
        return jnp.dot(a, b, preferred_element_type=F32)

    def nt(a, b):
        return lax.dot_general(a, b, (((1,), (1,)), ((), ())), preferred_element_type=F32)

    def tn(a, b):
        return lax.dot_general(a, b, (((0,), (0,)), ((), ())), preferred_element_type=F32)

    def cat(a, b):
        return jnp.concatenate([a, b], axis=0)

    def body(ci, carry):
        off = pl.multiple_of(ci * c, c)
        sl = pl.ds(off, c)
        inst = [(b, slice(g * gw, (g + 1) * gw)) for b in range(bb) for g in range(n_groups)]
        r_l, k_l, v_l, kk_l, al_l, rk_l, gi_l, gv_l, gm_l = ([] for _ in range(9))
        for b in range(bb):
            r = r_ref[b, sl, :].astype(F32)
            k = k_ref[b, sl, :].astype(F32)
            v = v_ref[b, sl, :].astype(F32)
            lw = lw_ref[b, sl, :]
            al = al_ref[b, sl, :]
            kk_raw = k * kk_ref[...]
            k = k * (1.0 + (al - 1.0) * ka_ref[...])
            rk = r * k * rk_ref[...]
            if t_valid is not None:
                ok = tb * (n_chunks * c) + off + lax.broadcasted_iota(jnp.int32, (c, 1), 0) < t_valid
                lw = jnp.where(ok, lw, 0.0)
                k = jnp.where(ok, k, 0.0)
                al = jnp.where(ok, al, 0.0)
            cum = _cumsum_time(tri_f, lw[None])[0]
            g_in, g_inv, g_m = jnp.exp(cum), jnp.exp(-cum), jnp.exp(cum - lw)
            for g in range(n_groups):
                ln = slice(g * gw, (g + 1) * gw)
                for lst, val in ((r_l, r), (k_l, k), (v_l, v), (kk_l, kk_raw), (al_l, al), (rk_l, rk),
                                 (gi_l, g_in), (gv_l, g_inv), (gm_l, g_m)):
                    lst.append(val[:, ln])
        n = len(inst)
        each = range(n)
        ss_l = [seg_sum(kk_l[i] * kk_l[i], pieces=2) for i in each]
        kk_l = [kk_l[i] / jnp.maximum(jnp.sqrt(ss_l[i]), 1e-12) for i in each]
        kt_l = [k_l[i] * gv_l[i] for i in each]
        bt_l = [kk_l[i] * al_l[i] * gv_l[i] for i in each]
        x_l = [cat(-kk_l[i] * gm_l[i], r_l[i] * gi_l[i]) for i in each]
        s_l = [s_scr[b, :, ln] for b, ln in inst]
        a_b = [nt(x_l[i], bd(bt_l[i], bd_c)) for i in each]
        a_k = [nt(x_l[i], bd(kt_l[i], bd_c)) for i in each]
        x_s = [nt(x_l[i], bd(s_l[i], bd_v)) for i in each]
        n_ab = [jnp.where(strict, a_b[i][:c], 0.0) for i in each]
        a_rb = [jnp.where(incl, a_b[i][c:], 0.0) for i in each]
        a_kk = [cat(jnp.where(strict, a_k[i][:c], 0.0), jnp.where(incl, a_k[i][c:], 0.0)) for i in each]
        a_v = [nn(a_kk[i], bd(v_l[i], bd_c)) for i in each]
        w_m = [x_s[i][:c] + a_v[i][:c] for i in each]
        t_m = [eye + n_ab[i] for i in each]
        p = n_ab
        if n_sq >= 1:
            p = [nn(p[i], bd(p[i], bd_p)) for i in each]
            for _ in range(n_sq - 1):
                both = [nn(cat(p[i], t_m[i]), bd(p[i], bd_p)) for i in each]
                p = [both[i][:c] for i in each]
                t_m = [t_m[i] + both[i][c:] for i in each]
            t_m = [t_m[i] + nn(t_m[i], bd(p[i], bd_p)) for i in each]
        u = [nn(t_m[i], bd(w_m[i], bd_c)) for i in each]
        y = [x_s[i][c:] + a_v[i][c:] for i in each]
        y_u = [nn(a_rb[i], bd(u[i], bd_c)) for i in each]
        full = [tn(cat(v_l[i], u[i]), cat(kt_l[i], bt_l[i])) for i in each]
        for i, (b, ln) in enumerate(inst):
            upd = sum(jnp.where(lane_head == h, full[i][h * C_HEAD:(h + 1) * C_HEAD], 0.0) for h in range(C_PACK))
            s_scr[b, :, ln] = (s_l[i] + upd) * gi_l[i][c - 1:c]
        y = [y[i] + y_u[i] for i in each]
        mu = [seg_sum(y[i]) * (1.0 / C_HEAD) for i in each]
        dev = [y[i] - mu[i] for i in each]
        var = [seg_sum(dev[i] * dev[i]) * (1.0 / C_HEAD) for i in each]
        bonus = [seg_sum(rk_l[i]) * v_l[i] for i in each]
        yn = [dev[i] * lax.rsqrt(var[i] + GN_EPS) for i in each]
        for b in range(bb):
            mine = range(b * n_groups, (b + 1) * n_groups)
            yn_b = jnp.concatenate([yn[i] for i in mine], axis=-1)
            bonus_b = jnp.concatenate([bonus[i] for i in mine], axis=-1)
            y_ref[b, sl, :] = (yn_b * lnw_ref[...] + lnb_ref[...] + bonus_b) * gate_ref[b, sl, :].astype(F32)
        return carry

    lax.fori_loop(0, n_chunks, body, 0)

    @pl.when(tb == pl.num_programs(1) - 1)
    def _():
        final = jnp.stack([jnp.stack([s_scr[b, :, h * C_HEAD:(h + 1) * C_HEAD] for h in range(C_HEADS)])
                           for b in range(bb)])
        _write_state_slab(sfin_ref, slot, aliased, final)


def _rwkv_mixer(r, k, v, gate, lw, al, vecs, s0, prev, *, layer, n_layers, bb, tblock, chunk, t_valid):
    bsz, t, d = r.shape
    st_out, st_shape, slot, aliased = _stacked_state_specs(prev, s0.shape, n_layers, layer, bb)
    tok = pl.BlockSpec((bb, tblock, d), lambda i, j: (i, j, 0))
    vec = pl.BlockSpec((1, d), lambda i, j: (0, 0))
    st = pl.BlockSpec((bb, C_HEADS, C_HEAD, C_HEAD), lambda i, j: (i, 0, 0, 0))
    seg = jnp.kron(jnp.eye(C_PACK, dtype=F32), jnp.ones((C_HEAD, C_HEAD), F32))
    in_specs = [tok] * 6 + [vec] * 5 + [pl.BlockSpec(seg.shape, lambda i, j: (0, 0)), st]
    args = [r, k, v, gate, lw, al, *vecs, seg, s0]
    aliases = {}
    if aliased:
        aliases = {len(args): 1}
        in_specs.append(pl.BlockSpec(memory_space=pl.ANY))
        args.append(prev)
    return pl.pallas_call(
        functools.partial(_rwkv_kernel, bb=bb, chunk=chunk, n_chunks=tblock // chunk, t_valid=t_valid,
                          slot=slot, aliased=aliased),
        grid=(bsz // bb, t // tblock),
        in_specs=in_specs, out_specs=[tok, st_out], out_shape=[jax.ShapeDtypeStruct((bsz, t, d), F32), st_shape],
        input_output_aliases=aliases,
        scratch_shapes=[pltpu.VMEM((bb, C_HEAD, d), F32)],
        compiler_params=_params(("parallel", "arbitrary")), name="rwkv7_mixer",
    )(*args)


def _s5_params(w, j):
    lr = jnp.minimum(w["ssm_lambda_re"][j], -1e-4)
    li = w["ssm_lambda_im"][j]
    step = jnp.exp(w["ssm_log_step"][j])[:, None]
    mag = jnp.exp(lr * step)
    ab_re = mag * jnp.cos(li * step)
    ab_im = mag * jnp.sin(li * step)
    den = lr * lr + li * li
    nr = ab_re - 1.0
    cr = (nr * lr + ab_im * li) / den
    ci = (ab_im * lr - nr * li) / den
    b_re, b_im = w["ssm_b_re"][j], w["ssm_b_im"][j]
    bb_re = cr[..., None] * b_re - ci[..., None] * b_im
    bb_im = cr[..., None] * b_im + ci[..., None] * b_re
    gpb = A_GROUPS // A_BLOCKS
    eye = jnp.eye(gpb, dtype=F32)

    def pack_in(bb):
        bb = bb.reshape(A_BLOCKS, gpb, A_STATE, A_GROUP)
        return jnp.einsum("bgph,gk->bghkp", bb, eye).reshape(A_BLOCKS, gpb * A_GROUP, gpb * A_STATE)

    def pack_out(c):
        c = c.reshape(A_BLOCKS, gpb, A_GROUP, A_STATE)
        return jnp.einsum("bghp,gk->bgpkh", c, eye).reshape(A_BLOCKS, gpb * A_STATE, gpb * A_GROUP)

    return dict(ab_re=ab_re.reshape(1, A_LANES), ab_im=ab_im.reshape(1, A_LANES),
                bb_re=pack_in(bb_re).astype(BF16), bb_im=pack_in(bb_im).astype(BF16),
                c_re=pack_out(w["ssm_c_re"][j]).astype(BF16), c_im=pack_out(w["ssm_c_im"][j]).astype(BF16),
                d=w["ssm_d"][j].reshape(1, A_WIDTH), glu_w=w["ssm_glu_w"][j],
                glu_b=w["ssm_glu_b"][j].reshape(1, A_WIDTH))


def _trunk(x, ssm_re, ssm_im, hgrn, wkv, shift, w, cfg):
    bsz, t, d = x.shape
    t_valid = cfg["t_valid"]
    m = bsz * t
    h = x.reshape(m, d)
    n_re, n_im, n_sh = [], [], []
    hg_all = wkv_all = None
    lbs = jax.nn.softmax(w["hgrn_lower_bounds"], axis=0)
    lbs = jnp.cumsum(lbs, axis=0) - lbs[0]
    v_first = None
    y = None
    depth = w["norm_w"].shape[0]
    for layer in range(depth):
        j = layer // 2
        nw = w["norm_w"][layer].reshape(1, d)
        last = layer == depth - 1
        final_nw = w["final_norm_w"].reshape(1, d) if last else None
        if layer % 2 == 0:
            proj = _norm_matmul(h, nw, w["even_w_in"][j], tm=cfg["tm"], tn=1024).reshape(bsz, t, -1)
            out_a, hr, hi = _s5_mixer(proj, ssm_re[j].reshape(bsz, A_LANES), ssm_im[j].reshape(bsz, A_LANES),
                                      _s5_params(w, j), bb=cfg["s5_bb"], tc=cfg["s5_tc"], t_valid=t_valid)
            out_b, hg_all = _hgrn_mixer(proj, hgrn, lbs[j].reshape(1, B_WIDTH),
                                        w["hgrn_norm_w"][j].reshape(1, B_DIM), hg_all, layer=j,
                                        n_layers=hgrn.shape[0], bb=cfg["hgrn_bb"], tblock=cfg["hgrn_tblock"],
                                        chunk=cfg["chunk"], t_valid=t_valid, use_lb=j > 0)
            res = _proj_residual(h, [out_a.reshape(m, A_WIDTH), out_b.reshape(m, B_WIDTH)], w["even_w_out"][j],
                                 final_nw, tm=cfg["tm"])
            n_re.append(hr.reshape(bsz, A_GROUPS, A_STATE))
            n_im.append(hi.reshape(bsz, A_GROUPS, A_STATE))
        else:
            row = lambda a: a.reshape(1, -1)
            lora_w = [row(w["rw_w0"][j]), w["rw_w1"][j], w["rw_w2"][j], row(w["rw_a0"][j]), w["rw_a1"][j],
                      w["rw_a2"][j], w["rw_g1"][j], w["rw_g2"][j]]
            if v_first is not None:
                lora_w += [row(w["rw_v0"][j - 1]), w["rw_v1"][j - 1], w["rw_v2"][j - 1]]
            r, k, v, gate, lw, al, x_last = _rwkv_in(h.reshape(bsz, t, d), nw, shift[j], w["rw_mix"][j],
                                                     w["rw_w_rkvz"][j], lora_w, v_first, bb=cfg["in_bb"],
                                                     tblock=cfg["in_tblock"], t_valid=t_valid,
                                                     act_dtype=cfg["act_dtype"])
            n_sh.append(x_last.reshape(bsz, d))
            if v_first is None:
                v_first = v
            vecs = [row(w[n][j]) for n in ("rw_k_k", "rw_k_a", "rw_r_k", "rw_ln_w", "rw_ln_b")]
            y_mix, wkv_all = _rwkv_mixer(r, k, v, gate, lw, al, vecs, wkv[j], wkv_all, layer=j,
                                         n_layers=wkv.shape[0], bb=cfg["bb"], tblock=cfg["tblock"],
                                         chunk=cfg["rwkv_chunk"], t_valid=t_valid)
            res = _proj_residual(h, [y_mix.reshape(m, d)], w["rw_w_o"][j], final_nw, tm=cfg["tm"])
        if last:
            h, y = res
        else:
            h = res[0]
    return (y.reshape(bsz, t, d), jnp.stack(n_re), jnp.stack(n_im), hg_all, wkv_all,
            jnp.stack(n_sh))


MATMUL_WEIGHTS = ("even_w_in", "even_w_out", "ssm_glu_w", "rw_w_rkvz", "rw_w1", "rw_w2", "rw_a1", "rw_a2",
                  "rw_v1", "rw_v2", "rw_g1", "rw_g2", "rw_w_o")
PROMPT_CFG = dict(t_valid=None, tm=1024, s5_bb=8, s5_tc=64, bb=2, tblock=128, chunk=32, rwkv_chunk=64,
                  hgrn_bb=8, hgrn_tblock=128, in_bb=1, in_tblock=512, act_dtype=BF16)
SAMPLE_PAD = 8
SAMPLE_CFG = dict(t_valid=4, tm=1024, s5_bb=32, s5_tc=SAMPLE_PAD, bb=8, tblock=SAMPLE_PAD, chunk=SAMPLE_PAD,
                  rwkv_chunk=SAMPLE_PAD, hgrn_bb=8, hgrn_tblock=SAMPLE_PAD, in_bb=16, in_tblock=SAMPLE_PAD,
                  act_dtype=F32)


def kernel(x_prompt, x_sample, state_ssm_re, state_ssm_im, state_hgrn, state_wkv, state_shift, norm_w, final_norm_w, even_w_in, even_w_out, ssm_lambda_re, ssm_lambda_im, ssm_log_step, ssm_b_re, ssm_b_im, ssm_c_re, ssm_c_im, ssm_d, ssm_glu_w, ssm_glu_b, hgrn_lower_bounds, hgrn_norm_w, rw_mix, rw_w_rkvz, rw_w0, rw_w1, rw_w2, rw_a0, rw_a1, rw_a2, rw_v0, rw_v1, rw_v2, rw_g1, rw_g2, rw_k_k, rw_k_a, rw_r_k, rw_ln_w, rw_ln_b, rw_w_o):
    w = dict(norm_w=norm_w, final_norm_w=final_norm_w, even_w_in=even_w_in, even_w_out=even_w_out,
             ssm_lambda_re=ssm_lambda_re, ssm_lambda_im=ssm_lambda_im, ssm_log_step=ssm_log_step,
             ssm_b_re=ssm_b_re, ssm_b_im=ssm_b_im, ssm_c_re=ssm_c_re, ssm_c_im=ssm_c_im, ssm_d=ssm_d,
             ssm_glu_w=ssm_glu_w, ssm_glu_b=ssm_glu_b, hgrn_lower_bounds=hgrn_lower_bounds,
             hgrn_norm_w=hgrn_norm_w, rw_mix=rw_mix, rw_w_rkvz=rw_w_rkvz, rw_w0=rw_w0, rw_w1=rw_w1,
             rw_w2=rw_w2, rw_a0=rw_a0, rw_a1=rw_a1, rw_a2=rw_a2, rw_v0=rw_v0, rw_v1=rw_v1, rw_v2=rw_v2,
             rw_g1=rw_g1, rw_g2=rw_g2, rw_k_k=rw_k_k, rw_k_a=rw_k_a, rw_r_k=rw_r_k,
             rw_ln_w=rw_ln_w, rw_ln_b=rw_ln_b, rw_w_o=rw_w_o)
    for name in MATMUL_WEIGHTS:
        w[name] = w[name].astype(BF16)
    n_even, n_odd = state_hgrn.shape[0], state_wkv.shape[0]
    bp, bs, ts = x_prompt.shape[0], x_sample.shape[0], x_sample.shape[1]
    assert ts == SAMPLE_CFG["t_valid"]
    z_re = jnp.zeros((n_even, bp, A_GROUPS, A_STATE), F32)
    z_hg = jnp.zeros((n_even, bp, B_HEADS, B_DIM, B_DIM), F32)
    z_wkv = jnp.zeros((n_odd, bp, C_HEADS, C_HEAD, C_HEAD), F32)
    z_sh = jnp.zeros((n_odd, bp, D_MODEL), F32)
    prompt = _trunk(x_prompt, z_re, z_re, z_hg, z_wkv, z_sh, w, PROMPT_CFG)
    xs = jnp.pad(x_sample, ((0, 0), (0, SAMPLE_PAD - ts), (0, 0)))
    sample = _trunk(xs, state_ssm_re, state_ssm_im, state_hgrn, state_wkv, state_shift, w, SAMPLE_CFG)
    y_sample = sample[0][:, :ts]
    return (prompt[0], y_sample, *prompt[1:], *sample[1:])
```

```python
import functools
import math

import jax
import jax.numpy as jnp
from jax import lax
from jax.experimental import pallas as pl
from jax.experimental.pallas import tpu as pltpu

F32 = jnp.float32
BF16 = jnp.bfloat16

D_MODEL = 1024
A_WIDTH = 512
A_GROUP = 16
A_GROUPS = 32
A_STATE = 64
A_LANES = A_GROUPS * A_STATE
A_BLOCKS = 4
B_WIDTH = 512
PROJ_U, PROJ_ZA, PROJ_Q, PROJ_I, PROJ_ZB = range(5)
B_HEADS = 4
B_DIM = 128
C_HEADS = 16
C_HEAD = 64
C_PACK = 4
DECAY_SCALE = math.exp(-0.5)
RMS_EPS = 1e-6
GN_EPS = 64e-5
VMEM_LIMIT = 56 * 1024 * 1024
LANES = 128


def _dot(a, b):
    return jnp.dot(a.astype(b.dtype), b, preferred_element_type=F32)


def _bdot(a, b, contract):
    return lax.dot_general(a, b, (contract, ((0,), (0,))), preferred_element_type=F32)


def _nn(a, b):
    return _bdot(a, b, ((2,), (1,)))


def _nt(a, b):
    return _bdot(a, b, ((2,), (2,)))


def _tn(a, b):
    return _bdot(a, b, ((1,), (1,)))


def _rms(x, w):
    return x * lax.rsqrt(jnp.mean(x * x, axis=-1, keepdims=True) + RMS_EPS) * w


def _sigmoid(x):
    return 1.0 / (1.0 + jnp.exp(-x))


def _silu(x):
    return x * _sigmoid(x)


def _params(sem):
    return pltpu.CompilerParams(dimension_semantics=sem, vmem_limit_bytes=VMEM_LIMIT)


def _stacked_state_specs(prev, state_shape, n_layers, layer, bb):
    tail = (0,) * (len(state_shape) - 1)
    shape = jax.ShapeDtypeStruct((n_layers, *state_shape), F32)
    block = (bb, *state_shape[1:])
    if prev is None:
        return pl.BlockSpec((n_layers, *block), lambda i, j: (0, i, *tail)), shape, layer, False
    return pl.BlockSpec((1, *block), lambda i, j: (layer, i, *tail)), shape, 0, True


def _write_state_slab(ref, slot, aliased, value):
    if not aliased:
        for other in range(ref.shape[0]):
            if other != slot:
                ref[other] = jnp.zeros(ref.shape[1:], F32)
    ref[slot] = value


def _tri_masks(c):
    row = lax.broadcasted_iota(jnp.int32, (c, c), 0)
    col = lax.broadcasted_iota(jnp.int32, (c, c), 1)
    return row >= col, row > col, row == col


def _split_heads(x, width):
    bb, _, d = x.shape
    return jnp.stack([x[i, :, h * width:(h + 1) * width] for i in range(bb) for h in range(d // width)], axis=0)


def _merge_heads(x, bb):
    h = x.shape[0] // bb
    return jnp.stack([jnp.concatenate([x[i * h + j] for j in range(h)], axis=-1) for i in range(bb)], axis=0)


def _cumsum_time(tri_f, x):
    hi = x.astype(BF16).astype(F32)
    rest = x - hi
    mid = rest.astype(BF16).astype(F32)
    lo = rest - mid
    return jnp.stack([sum(jnp.dot(tri_f, p[i], preferred_element_type=F32) for p in (hi, mid, lo))
                      for i in range(x.shape[0])], axis=0)


def _norm_matmul_kernel(x_ref, nw_ref, w_ref, rest_ref, f_ref):
    res = _dot(_rms(x_ref[...], nw_ref[...]), w_ref[...])
    n_rest = rest_ref.shape[-1]
    rest_ref[...] = res[:, :n_rest].astype(rest_ref.dtype)
    f_ref[...] = res[:, n_rest:]


def _norm_matmul(x, nw, w, *, tm, n_f32, act_dtype):
    m, k = x.shape
    n = w.shape[1]
    row = lambda width: pl.BlockSpec((tm, width), lambda i: (i, 0))
    return pl.pallas_call(
        _norm_matmul_kernel, grid=(m // tm,),
        in_specs=[row(k), pl.BlockSpec((1, k), lambda i: (0, 0)), pl.BlockSpec((k, n), lambda i: (0, 0))],
        out_specs=[row(n - n_f32), row(n_f32)],
        out_shape=[jax.ShapeDtypeStruct((m, n - n_f32), act_dtype), jax.ShapeDtypeStruct((m, n_f32), F32)],
        compiler_params=_params(("parallel",)), name="norm_in_proj",
    )(x, nw, w)


def _proj_residual_kernel(*refs, n_x, final_norm):
    h_ref, x_refs, w_ref = refs[0], refs[1:1 + n_x], refs[1 + n_x]
    x = jnp.concatenate([r[...] for r in x_refs], axis=-1) if n_x > 1 else x_refs[0][...]
    h = h_ref[...] + _dot(x, w_ref[...])
    if final_norm:
        nw_ref, h_out, y_out = refs[2 + n_x:]
        y_out[...] = _rms(h, nw_ref[...])
    else:
        h_out = refs[2 + n_x]
    h_out[...] = h


def _proj_residual(h, xs, w, final_nw=None, *, tm):
    m, n = h.shape
    final_norm = final_nw is not None
    row = lambda width: pl.BlockSpec((tm, width), lambda i: (i, 0))
    in_specs = [row(n)] + [row(x.shape[1]) for x in xs] + [pl.BlockSpec(w.shape, lambda i: (0, 0))]
    args = [h, *xs, w]
    out_specs, out_shape = [row(n)], [jax.ShapeDtypeStruct((m, n), F32)]
    if final_norm:
        in_specs.append(pl.BlockSpec((1, n), lambda i: (0, 0)))
        args.append(final_nw)
        out_specs.append(row(n))
        out_shape.append(jax.ShapeDtypeStruct((m, n), F32))
    return pl.pallas_call(
        functools.partial(_proj_residual_kernel, n_x=len(xs), final_norm=final_norm), grid=(m // tm,),
        in_specs=in_specs, out_specs=out_specs, out_shape=out_shape,
        compiler_params=_params(("parallel",)), name="out_proj_residual",
    )(*args)


def _s5_kernel(u_ref, za_ref, h0r_ref, h0i_ref, ar_ref, ai_ref, bbr_ref, bbi_ref, cr_ref, ci_ref,
               d_ref, gw_ref, gb_ref, out_ref, hfr_ref, hfi_ref, hre, him, bur, bui, st_r, st_i, wbm, wtm,
               *, bsz, tc, n_steps):
    i = pl.program_id(1)
    rows = bsz * tc
    n_tiles = A_LANES // LANES
    u_tiles = A_WIDTH // LANES
    per = n_tiles // A_BLOCKS
    assert A_WIDTH // A_BLOCKS == LANES

    @pl.when(i == 0)
    def _():
        for cb in range(n_tiles):
            st_r[cb] = h0r_ref[:, cb * LANES:(cb + 1) * LANES]
            st_i[cb] = h0i_ref[:, cb * LANES:(cb + 1) * LANES]

    def bm(t):
        return pl.ds(t, bsz, stride=tc)

    def tm(t):
        return pl.ds(pl.multiple_of(t * bsz, bsz), bsz)

    def reorder(src, dst, src_rows, dst_rows):
        def body(t, carry):
            for q in range(u_tiles):
                dst.at[q][dst_rows(t), :] = src.at[q][src_rows(t), :]
            return carry
        lax.fori_loop(0, tc, body, 0, unroll=min(tc, 8))

    u_bm = u_ref[...].astype(F32).reshape(rows, A_WIDTH)
    for q in range(u_tiles):
        wbm[q] = u_bm[:, q * LANES:(q + 1) * LANES]
    reorder(wbm, wtm, bm, tm)

    for j in range(A_BLOCKS):
        bu_r = _dot(wtm[j], bbr_ref[j])
        bu_i = _dot(wtm[j], bbi_ref[j])
        for q in range(per):
            bur[j * per + q] = bu_r[:, q * LANES:(q + 1) * LANES]
            bui[j * per + q] = bu_i[:, q * LANES:(q + 1) * LANES]

    def advance(t, prev):
        new = []
        for cb in range(n_tiles):
            lanes = slice(cb * LANES, (cb + 1) * LANES)
            ar, ai = ar_ref[:, lanes], ai_ref[:, lanes]
            pr, pi = prev[2 * cb], prev[2 * cb + 1]
            hr = ar * pr - ai * pi + bur[cb, tm(t), :]
            hi = ar * pi + ai * pr + bui[cb, tm(t), :]
            hre[cb, tm(t), :] = hr
            him[cb, tm(t), :] = hi
            new += [hr, hi]
        return tuple(new)

    state = tuple(ref[cb] for cb in range(n_tiles) for ref in (st_r, st_i))
    if n_steps <= 8:
        for t in range(n_steps):
            state = advance(t, state)
    else:
        state = lax.fori_loop(0, n_steps, advance, state)
    for cb in range(n_tiles):
        st_r[cb] = state[2 * cb]
        st_i[cb] = state[2 * cb + 1]
    if n_steps < tc:
        pad = pl.ds(n_steps * bsz, (tc - n_steps) * bsz)
        for cb in range(n_tiles):
            hre[cb, pad, :] = bur[cb, pad, :]
            him[cb, pad, :] = bui[cb, pad, :]

    def block(ref, j):
        return jnp.concatenate([ref[j * per + q] for q in range(per)], axis=-1)

    u_tm = jnp.concatenate([wtm[q] for q in range(u_tiles)], axis=-1)
    y = jnp.concatenate(
        [_dot(block(hre, j), cr_ref[j]) - _dot(block(him, j), ci_ref[j]) for j in range(A_BLOCKS)],
        axis=-1) + d_ref[...] * u_tm
    y = 0.5 * y * (1.0 + jnp.tanh(math.sqrt(2.0 / math.pi) * (y + 0.044715 * (y * y * y))))
    y = y * _sigmoid(_dot(y, gw_ref[...]) + gb_ref[...])
    for q in range(u_tiles):
        wtm[q] = y[:, q * LANES:(q + 1) * LANES]
    reorder(wtm, wbm, tm, bm)
    y_bm = jnp.concatenate([wbm[q] for q in range(u_tiles)], axis=-1)
    out = y_bm * _silu(za_ref[...].astype(F32).reshape(rows, A_WIDTH))
    out_ref[...] = out.reshape(bsz, tc, A_WIDTH).astype(out_ref.dtype)

    @pl.when(i == pl.num_programs(1) - 1)
    def _():
        for cb in range(n_tiles):
            hfr_ref[:, cb * LANES:(cb + 1) * LANES] = st_r[cb]
            hfi_ref[:, cb * LANES:(cb + 1) * LANES] = st_i[cb]


def _s5_mixer(proj, h0r, h0i, sp, *, bb, tc, t_valid):
    bsz, t, _ = proj.shape
    n_steps = tc if t_valid is None else t_valid
    full = lambda a: pl.BlockSpec(a.shape, lambda b, i: (0,) * a.ndim)
    col = lambda c: pl.BlockSpec((bb, tc, A_WIDTH), lambda b, i: (b, i, c))
    st = pl.BlockSpec((bb, A_LANES), lambda b, i: (b, 0))
    assert (PROJ_U, PROJ_ZA) == (0, 1)
    weights = [sp["ab_re"], sp["ab_im"], sp["bb_re"], sp["bb_im"], sp["c_re"], sp["c_im"],
               sp["d"], sp["glu_w"], sp["glu_b"]]
    state = jax.ShapeDtypeStruct((bsz, A_LANES), F32)
    return pl.pallas_call(
        functools.partial(_s5_kernel, bsz=bb, tc=tc, n_steps=n_steps), grid=(bsz // bb, t // tc),
        in_specs=[col(0), col(1), st, st] + [full(a) for a in weights],
        out_specs=[col(0), st, st],
        out_shape=[jax.ShapeDtypeStruct((bsz, t, A_WIDTH), proj.dtype), state, state],
        scratch_shapes=[pltpu.VMEM((A_LANES // LANES, bb * tc, LANES), F32)] * 4
        + [pltpu.VMEM((A_LANES // LANES, bb, LANES), F32)] * 2
        + [pltpu.VMEM((A_WIDTH // LANES, bb * tc, LANES), F32)] * 2,
        compiler_params=_params(("parallel", "arbitrary")), name="s5_mixer",
    )(proj, proj, h0r, h0i, *weights)


def _hgrn_kernel(*refs, bb, chunk, n_chunks, t_valid, use_lb, slot, aliased):
    q_ref, f_ref, i_ref, zb_ref, lb_ref, nw_ref, s0_ref = refs[:7]
    out_ref, sfin_ref, s_scr = refs[-3:]
    c = chunk
    tb = pl.program_id(1)

    @pl.when(tb == 0)
    def _():
        s_scr[...] = jnp.swapaxes(s0_ref[0].reshape(bb * B_HEADS, B_DIM, B_DIM), 1, 2)

    incl, _, _ = _tri_masks(c)
    tri_f = incl.astype(F32)

    def body(ci, carry):
        off = pl.multiple_of(ci * c, c)
        sl = pl.ds(off, c)
        q = _silu(q_ref[:, sl, :].astype(F32))
        f = f_ref[:, sl, :]
        if use_lb:
            lb = lb_ref[...]
            log_f = jnp.log(lb + (1.0 - lb) * _sigmoid(f))
            k = (1.0 - lb) * _sigmoid(-f)
        else:
            log_f = jnp.minimum(f, 0.0) - jnp.log1p(jnp.exp(-jnp.abs(f)))
            k = _sigmoid(-f)
        if t_valid is not None:
            tok = tb * (n_chunks * c) + off + lax.broadcasted_iota(jnp.int32, (1, c, 1), 1)
            log_f = jnp.where(tok < t_valid, log_f, 0.0)
            k = jnp.where(tok < t_valid, k, 0.0)
        cum = _cumsum_time(tri_f, log_f)
        last = cum[:, c - 1:c, :]
        mid = cum[:, c // 2 - 1:c // 2, :]
        qh = _split_heads(q * jnp.exp(cum), B_DIM)
        qm = _split_heads(q * jnp.exp(cum - mid), B_DIM)
        kh = _split_heads(k * jnp.exp(mid - cum), B_DIM)
        ke = _split_heads(k * jnp.exp(last - cum), B_DIM)
        vh = _split_heads(i_ref[:, sl, :].astype(F32), B_DIM)
        gl = _split_heads(jnp.exp(last), B_DIM)
        s = s_scr[...]
        att = jnp.where(incl[None], _nt(qm, kh), 0.0)
        o = _nn(att, vh) + _nt(qh, s)
        s_scr[...] = s * gl + _tn(vh, ke)
        o = o * lax.rsqrt(jnp.mean(o * o, axis=-1, keepdims=True) + RMS_EPS) * nw_ref[...]
        out_ref[:, sl, :] = (_merge_heads(o, bb) * _silu(zb_ref[:, sl, :].astype(F32))).astype(out_ref.dtype)
        return carry

    lax.fori_loop(0, n_chunks, body, 0)

    @pl.when(tb == pl.num_programs(1) - 1)
    def _():
        _write_state_slab(sfin_ref, slot, aliased,
                          jnp.swapaxes(s_scr[...], 1, 2).reshape(bb, B_HEADS, B_DIM, B_DIM))


def _hgrn_mixer(proj, proj_f, s0, lb, nw, prev, *, layer, n_layers, bb, tblock, chunk, t_valid, use_lb):
    bsz, t, _ = proj.shape
    st_out, st_shape, slot, aliased = _stacked_state_specs(prev, s0.shape[1:], n_layers, layer, bb)
    extra = ([pl.BlockSpec(memory_space=pl.ANY)], [prev], {7: 1}) if aliased else ([], [], {})
    col = lambda c: pl.BlockSpec((bb, tblock, B_WIDTH), lambda i, j: (i, j, c))
    vec = lambda a: pl.BlockSpec(a.shape, lambda i, j: (0, 0))
    st = pl.BlockSpec((1, bb, B_HEADS, B_DIM, B_DIM), lambda i, j: (layer, i, 0, 0, 0))
    return pl.pallas_call(
        functools.partial(_hgrn_kernel, bb=bb, chunk=chunk, n_chunks=tblock // chunk, t_valid=t_valid,
                          use_lb=use_lb, slot=slot, aliased=aliased),
        grid=(bsz // bb, t // tblock),
        in_specs=[col(PROJ_Q), col(0), col(PROJ_I), col(PROJ_ZB), vec(lb), vec(nw), st] + extra[0],
        out_specs=[pl.BlockSpec((bb, tblock, B_WIDTH), lambda i, j: (i, j, 0)), st_out],
        out_shape=[jax.ShapeDtypeStruct((bsz, t, B_WIDTH), proj.dtype), st_shape],
        scratch_shapes=[pltpu.VMEM((bb * B_HEADS, B_DIM, B_DIM), F32)],
        input_output_aliases=extra[2],
        compiler_params=_params(("parallel", "arbitrary")), name="hgrn2_mixer",
    )(proj, proj_f, proj, proj, lb, nw, s0, *extra[1])


def _rwkv_in_kernel(*refs, tblock, last_row, vres):
    h_ref, nw_ref, sh_ref, mix_ref, wp_ref, w0, w1, w2, a0, a1, a2, g1, g2 = refs[:13]
    if vres:
        v0, v1, v2, vf_ref = refs[13:17]
    r_ref, k_ref, v_ref, gate_ref, lw_ref, al_ref, last_ref, carry = refs[-8:]

    @pl.when(pl.program_id(1) == 0)
    def _():
        carry[...] = sh_ref[...]

    xn = _rms(h_ref[...], nw_ref[...])
    row = lax.broadcasted_iota(jnp.int32, (1, tblock, 1), 1)
    xx = jnp.where(row == 0, carry[...], pltpu.roll(xn, 1, axis=1)) - xn
    carry[...] = xn[:, tblock - 1:tblock, :]
    last_ref[...] = xn[:, last_row:last_row + 1, :]
    shape = xn.shape
    xn = xn.reshape(-1, shape[-1])
    xx = xx.reshape(-1, shape[-1])
    mixed = lambda m: xn + xx * mix_ref[m:m + 1, :]
    store = lambda ref, val: ref.__setitem__(Ellipsis, val.reshape(shape).astype(ref.dtype))
    x_v, x_g = mixed(3), mixed(5)
    store(r_ref, _dot(mixed(0), wp_ref[0]))
    store(k_ref, _dot(mixed(2), wp_ref[1]))
    v = _dot(x_v, wp_ref[2])
    if vres:
        v_first = vf_ref[...].reshape(v.shape).astype(F32)
        v = v + (v_first - v) * _sigmoid(v0[...] + _dot(_dot(x_v, v1[...]), v2[...]))
    store(v_ref, v)
    store(gate_ref, _dot(_sigmoid(_dot(x_g, g1[...])), g2[...]) * _silu(_dot(x_g, wp_ref[3])))
    store(lw_ref, -DECAY_SCALE * _sigmoid(w0[...] + _dot(jnp.tanh(_dot(mixed(1), w1[...])), w2[...])))
    store(al_ref, _sigmoid(a0[...] + _dot(_dot(mixed(4), a1[...]), a2[...])))


def _rwkv_in(h, nw, shift, mix, wp, lora_w, v_first, *, bb, tblock, t_valid, act_dtype):
    bsz, t, d = h.shape
    vres = v_first is not None
    tok = pl.BlockSpec((bb, tblock, d), lambda i, j: (i, j, 0))
    one = pl.BlockSpec((bb, 1, d), lambda i, j: (i, 0, 0))
    full = lambda a: pl.BlockSpec(a.shape, lambda i, j: (0,) * a.ndim)
    in_specs = [tok, full(nw), one, full(mix), full(wp)] + [full(a) for a in lora_w]
    args = [h, nw, shift.reshape(bsz, 1, d), mix, wp, *lora_w]
    if vres:
        in_specs.append(tok)
        args.append(v_first)
    act = lambda dt: jax.ShapeDtypeStruct((bsz, t, d), dt)
    last_row = (tblock if t_valid is None else t_valid) - 1
    return pl.pallas_call(
        functools.partial(_rwkv_in_kernel, tblock=tblock, last_row=last_row, vres=vres),
        grid=(bsz // bb, t // tblock),
        in_specs=in_specs, out_specs=[tok] * 6 + [one],
        out_shape=[act(act_dtype)] * 4 + [act(F32)] * 2 + [jax.ShapeDtypeStruct((bsz, 1, d), F32)],
        scratch_shapes=[pltpu.VMEM((bb, 1, d), F32)],
        compiler_params=_params(("parallel", "arbitrary")), name="rwkv_in_proj",
    )(*args)


def _rwkv_kernel(*refs, bb, chunk, n_chunks, t_valid, slot, aliased):
    r_ref, k_ref, v_ref, gate_ref, lw_ref, al_ref, kk_ref, ka_ref, rk_ref, lnw_ref, lnb_ref, seg_ref, s0_ref = refs[:13]
    y_ref, sfin_ref, s_scr = refs[-3:]
    c = chunk
    gw = C_PACK * C_HEAD
    aw = C_PACK * c
    n_groups = D_MODEL // gw
    n_sq = int(math.log2(c)) - 1
    tb = pl.program_id(1)

    @pl.when(tb == 0)
    def _():
        for b in range(bb):
            s_scr[b] = jnp.concatenate([s0_ref[b, h] for h in range(C_HEADS)], axis=-1)

    row = lax.broadcasted_iota(jnp.int32, (c, aw), 0)
    col = lax.broadcasted_iota(jnp.int32, (c, aw), 1) % c
    incl, strict = row >= col, row > col
    eye = (row == col).astype(F32)
    tri_f = _tri_masks(c)[0].astype(F32)

    def bd_mask(rows_per_head, lanes, lanes_per_head):
        r = lax.broadcasted_iota(jnp.int32, (C_PACK * rows_per_head, lanes), 0) // rows_per_head
        l = lax.broadcasted_iota(jnp.int32, (C_PACK * rows_per_head, lanes), 1) // lanes_per_head
        return r == l

    bd_c, bd_v, bd_p = bd_mask(c, gw, C_HEAD), bd_mask(C_HEAD, gw, C_HEAD), bd_mask(c, aw, c)
    lane_head = lax.broadcasted_iota(jnp.int32, (1, gw), 1) // C_HEAD
    seg = seg_ref[...]

    def bd(y, mask):
        return jnp.where(mask, jnp.tile(y, (C_PACK, 1)), 0.0)

    def seg_sum(x, pieces=1):
        total = None
        for _ in range(pieces):
            hi = x.astype(BF16).astype(F32)
            part = jnp.dot(hi, seg, preferred_element_type=F32)
            total = part if total is None else total + part
            x = x - hi
        return total

    def nn(a, b):
        return jnp.dot(a, b, preferred_element_type=F32)

    def nt(a, b):
        return lax.dot_general(a, b, (((1,), (1,)), ((), ())), preferred_element_type=F32)

    def tn(a, b):
        return lax.dot_general(a, b, (((0,), (0,)), ((), ())), preferred_element_type=F32)

    def cat(a, b):
        return jnp.concatenate([a, b], axis=0)

    def body(ci, carry):
        off = pl.multiple_of(ci * c, c)
        sl = pl.ds(off, c)
        inst = [(b, slice(g * gw, (g + 1) * gw)) for b in range(bb) for g in range(n_groups)]
        r_l, k_l, v_l, kk_l, al_l, rk_l, gi_l, gv_l, gm_l = ([] for _ in range(9))
        for b in range(bb):
            r = r_ref[b, sl, :].astype(F32)
            k = k_ref[b, sl, :].astype(F32)
            v = v_ref[b, sl, :].astype(F32)
            lw = lw_ref[b, sl, :]
            al = al_ref[b, sl, :]
            kk_raw = k * kk_ref[...]
            k = k * (1.0 + (al - 1.0) * ka_ref[...])
            rk = r * k * rk_ref[...]
            if t_valid is not None:
                ok = tb * (n_chunks * c) + off + lax.broadcasted_iota(jnp.int32, (c, 1), 0) < t_valid
                lw = jnp.where(ok, lw, 0.0)
                k = jnp.where(ok, k, 0.0)
                al = jnp.where(ok, al, 0.0)
            cum = _cumsum_time(tri_f, lw[None])[0]
            g_in, g_inv, g_m = jnp.exp(cum), jnp.exp(-cum), jnp.exp(cum - lw)
            for g in range(n_groups):
                ln = slice(g * gw, (g + 1) * gw)
                for lst, val in ((r_l, r), (k_l, k), (v_l, v), (kk_l, kk_raw), (al_l, al), (rk_l, rk),
                                 (gi_l, g_in), (gv_l, g_inv), (gm_l, g_m)):
                    lst.append(val[:, ln])
        n = len(inst)
        each = range(n)
        ss_l = [seg_sum(kk_l[i] * kk_l[i], pieces=2) for i in each]
        kk_l = [kk_l[i] / jnp.maximum(jnp.sqrt(ss_l[i]), 1e-12) for i in each]
        kt_l = [k_l[i] * gv_l[i] for i in each]
        bt_l = [kk_l[i] * al_l[i] * gv_l[i] for i in each]
        x_l = [cat(-kk_l[i] * gm_l[i], r_l[i] * gi_l[i]) for i in each]
        s_l = [s_scr[b, :, ln] for b, ln in inst]
        a_b = [nt(x_l[i], bd(bt_l[i], bd_c)) for i in each]
        a_k = [nt(x_l[i], bd(kt_l[i], bd_c)) for i in each]
        x_s = [nt(x_l[i], bd(s_l[i], bd_v)) for i in each]
        n_ab = [jnp.where(strict, a_b[i][:c], 0.0) for i in each]
        a_rb = [jnp.where(incl, a_b[i][c:], 0.0) for i in each]
        a_kk = [cat(jnp.where(strict, a_k[i][:c], 0.0), jnp.where(incl, a_k[i][c:], 0.0)) for i in each]
        a_v = [nn(a_kk[i], bd(v_l[i], bd_c)) for i in each]
        w_m = [x_s[i][:c] + a_v[i][:c] for i in each]
        t_m = [eye + n_ab[i] for i in each]
        p = n_ab
        if n_sq >= 1:
            p = [nn(p[i], bd(p[i], bd_p)) for i in each]
            for _ in range(n_sq - 1):
                both = [nn(cat(p[i], t_m[i]), bd(p[i], bd_p)) for i in each]
                p = [both[i][:c] for i in each]
                t_m = [t_m[i] + both[i][c:] for i in each]
            t_m = [t_m[i] + nn(t_m[i], bd(p[i], bd_p)) for i in each]
        u = [nn(t_m[i], bd(w_m[i], bd_c)) for i in each]
        y = [x_s[i][c:] + a_v[i][c:] for i in each]
        y_u = [nn(a_rb[i], bd(u[i], bd_c)) for i in each]
        full = [tn(cat(v_l[i], u[i]), cat(kt_l[i], bt_l[i])) for i in each]
        for i, (b, ln) in enumerate(inst):
            upd = sum(jnp.where(lane_head == h, full[i][h * C_HEAD:(h + 1) * C_HEAD], 0.0) for h in range(C_PACK))
            s_scr[b, :, ln] = (s_l[i] + upd) * gi_l[i][c - 1:c]
        y = [y[i] + y_u[i] for i in each]
        mu = [seg_sum(y[i]) * (1.0 / C_HEAD) for i in each]
        dev = [y[i] - mu[i] for i in each]
        var = [seg_sum(dev[i] * dev[i]) * (1.0 / C_HEAD) for i in each]
        bonus = [seg_sum(rk_l[i]) * v_l[i] for i in each]
        yn = [dev[i] * lax.rsqrt(var[i] + GN_EPS) for i in each]
        for b in range(bb):
            mine = range(b * n_groups, (b + 1) * n_groups)
            yn_b = jnp.concatenate([yn[i] for i in mine], axis=-1)
            bonus_b = jnp.concatenate([bonus[i] for i in mine], axis=-1)
            y_ref[b, sl, :] = ((yn_b * lnw_ref[...] + lnb_ref[...] + bonus_b)
                               * gate_ref[b, sl, :].astype(F32)).astype(y_ref.dtype)
        return carry

    lax.fori_loop(0, n_chunks, body, 0)

    @pl.when(tb == pl.num_programs(1) - 1)
    def _():
        final = jnp.stack([jnp.stack([s_scr[b, :, h * C_HEAD:(h + 1) * C_HEAD] for h in range(C_HEADS)])
                           for b in range(bb)])
        _write_state_slab(sfin_ref, slot, aliased, final)


def _rwkv_mixer(r, k, v, gate, lw, al, vecs, s0, prev, *, layer, n_layers, bb, tblock, chunk, t_valid):
    bsz, t, d = r.shape
    st_out, st_shape, slot, aliased = _stacked_state_specs(prev, s0.shape, n_layers, layer, bb)
    tok = pl.BlockSpec((bb, tblock, d), lambda i, j: (i, j, 0))
    vec = pl.BlockSpec((1, d), lambda i, j: (0, 0))
    st = pl.BlockSpec((bb, C_HEADS, C_HEAD, C_HEAD), lambda i, j: (i, 0, 0, 0))
    seg = jnp.kron(jnp.eye(C_PACK, dtype=F32), jnp.ones((C_HEAD, C_HEAD), F32))
    in_specs = [tok] * 6 + [vec] * 5 + [pl.BlockSpec(seg.shape, lambda i, j: (0, 0)), st]
    args = [r, k, v, gate, lw, al, *vecs, seg, s0]
    aliases = {}
    if aliased:
        aliases = {len(args): 1}
        in_specs.append(pl.BlockSpec(memory_space=pl.ANY))
        args.append(prev)
    return pl.pallas_call(
        functools.partial(_rwkv_kernel, bb=bb, chunk=chunk, n_chunks=tblock // chunk, t_valid=t_valid,
                          slot=slot, aliased=aliased),
        grid=(bsz // bb, t // tblock),
        in_specs=in_specs, out_specs=[tok, st_out], out_shape=[jax.ShapeDtypeStruct((bsz, t, d), r.dtype), st_shape],
        input_output_aliases=aliases,
        scratch_shapes=[pltpu.VMEM((bb, C_HEAD, d), F32)],
        compiler_params=_params(("parallel", "arbitrary")), name="rwkv7_mixer",
    )(*args)


def _s5_params(w, j):
    lr = jnp.minimum(w["ssm_lambda_re"][j], -1e-4)
    li = w["ssm_lambda_im"][j]
    step = jnp.exp(w["ssm_log_step"][j])[:, None]
    mag = jnp.exp(lr * step)
    ab_re = mag * jnp.cos(li * step)
    ab_im = mag * jnp.sin(li * step)
    den = lr * lr + li * li
    nr = ab_re - 1.0
    cr = (nr * lr + ab_im * li) / den
    ci = (ab_im * lr - nr * li) / den
    b_re, b_im = w["ssm_b_re"][j], w["ssm_b_im"][j]
    bb_re = cr[..., None] * b_re - ci[..., None] * b_im
    bb_im = cr[..., None] * b_im + ci[..., None] * b_re
    gpb = A_GROUPS // A_BLOCKS
    eye = jnp.eye(gpb, dtype=F32)

    def pack_in(bb):
        bb = bb.reshape(A_BLOCKS, gpb, A_STATE, A_GROUP)
        return jnp.einsum("bgph,gk->bghkp", bb, eye).reshape(A_BLOCKS, gpb * A_GROUP, gpb * A_STATE)

    def pack_out(c):
        c = c.reshape(A_BLOCKS, gpb, A_GROUP, A_STATE)
        return jnp.einsum("bghp,gk->bgpkh", c, eye).reshape(A_BLOCKS, gpb * A_STATE, gpb * A_GROUP)

    return dict(ab_re=ab_re.reshape(1, A_LANES), ab_im=ab_im.reshape(1, A_LANES),
                bb_re=pack_in(bb_re).astype(BF16), bb_im=pack_in(bb_im).astype(BF16),
                c_re=pack_out(w["ssm_c_re"][j]).astype(BF16), c_im=pack_out(w["ssm_c_im"][j]).astype(BF16),
                d=w["ssm_d"][j].reshape(1, A_WIDTH), glu_w=w["ssm_glu_w"][j],
                glu_b=w["ssm_glu_b"][j].reshape(1, A_WIDTH))


def _trunk(x, ssm_re, ssm_im, hgrn, wkv, shift, w, prep, cfg):
    bsz, t, d = x.shape
    t_valid = cfg["t_valid"]
    m = bsz * t
    h = x.reshape(m, d)
    n_re, n_im, n_sh = [], [], []
    hg_all = wkv_all = None
    lbs = prep["lbs"]
    v_first = None
    y = None
    depth = w["norm_w"].shape[0]
    for layer in range(depth):
        j = layer // 2
        nw = w["norm_w"][layer].reshape(1, d)
        last = layer == depth - 1
        final_nw = w["final_norm_w"].reshape(1, d) if last else None
        if layer % 2 == 0:
            proj, proj_f = _norm_matmul(h, nw, prep["w_in"][j], tm=cfg["tm"] // 2, n_f32=B_WIDTH,
                                        act_dtype=cfg["act_dtype"])
            proj, proj_f = proj.reshape(bsz, t, -1), proj_f.reshape(bsz, t, -1)
            out_a, hr, hi = _s5_mixer(proj, ssm_re[j].reshape(bsz, A_LANES), ssm_im[j].reshape(bsz, A_LANES),
                                      prep["s5"][j], bb=cfg["s5_bb"], tc=cfg["s5_tc"], t_valid=t_valid)
            out_b, hg_all = _hgrn_mixer(proj, proj_f, hgrn, lbs[j].reshape(1, B_WIDTH),
                                        w["hgrn_norm_w"][j].reshape(1, B_DIM), hg_all, layer=j,
                                        n_layers=hgrn.shape[0], bb=cfg["hgrn_bb"], tblock=cfg["hgrn_tblock"],
                                        chunk=cfg["chunk"], t_valid=t_valid, use_lb=j > 0)
            res = _proj_residual(h, [out_a.reshape(m, A_WIDTH), out_b.reshape(m, B_WIDTH)], w["even_w_out"][j],
                                 final_nw, tm=cfg["tm"])
            n_re.append(hr.reshape(bsz, A_GROUPS, A_STATE))
            n_im.append(hi.reshape(bsz, A_GROUPS, A_STATE))
        else:
            row = lambda a: a.reshape(1, -1)
            lora_w = [row(w["rw_w0"][j]), w["rw_w1"][j], w["rw_w2"][j], row(w["rw_a0"][j]), w["rw_a1"][j],
                      w["rw_a2"][j], w["rw_g1"][j], w["rw_g2"][j]]
            if v_first is not None:
                lora_w += [row(w["rw_v0"][j - 1]), w["rw_v1"][j - 1], w["rw_v2"][j - 1]]
            r, k, v, gate, lw, al, x_last = _rwkv_in(h.reshape(bsz, t, d), nw, shift[j], w["rw_mix"][j],
                                                     w["rw_w_rkvz"][j], lora_w, v_first, bb=cfg["in_bb"],
                                                     tblock=cfg["in_tblock"], t_valid=t_valid,
                                                     act_dtype=cfg["act_dtype"])
            n_sh.append(x_last.reshape(bsz, d))
            if v_first is None:
                v_first = v
            vecs = [row(w[n][j]) for n in ("rw_k_k", "rw_k_a", "rw_r_k", "rw_ln_w", "rw_ln_b")]
            y_mix, wkv_all = _rwkv_mixer(r, k, v, gate, lw, al, vecs, wkv[j], wkv_all, layer=j,
                                         n_layers=wkv.shape[0], bb=cfg["bb"], tblock=cfg["tblock"],
                                         chunk=cfg["rwkv_chunk"], t_valid=t_valid)
            res = _proj_residual(h, [y_mix.reshape(m, d)], w["rw_w_o"][j], final_nw, tm=cfg["tm"])
        if last:
            h, y = res
        else:
            h = res[0]
    return (y.reshape(bsz, t, d), jnp.stack(n_re), jnp.stack(n_im), hg_all, wkv_all,
            jnp.stack(n_sh))


MATMUL_WEIGHTS = ("even_w_in", "even_w_out", "ssm_glu_w", "rw_w_rkvz", "rw_w1", "rw_w2", "rw_a1", "rw_a2",
                  "rw_v1", "rw_v2", "rw_g1", "rw_g2", "rw_w_o")
PROMPT_CFG = dict(t_valid=None, tm=1024, s5_bb=8, s5_tc=64, bb=2, tblock=128, chunk=32, rwkv_chunk=64,
                  hgrn_bb=8, hgrn_tblock=128, in_bb=1, in_tblock=512, act_dtype=BF16)
SAMPLE_PAD = 8
SAMPLE_CFG = dict(t_valid=4, tm=1024, s5_bb=32, s5_tc=SAMPLE_PAD, bb=8, tblock=SAMPLE_PAD, chunk=SAMPLE_PAD,
                  rwkv_chunk=SAMPLE_PAD, hgrn_bb=8, hgrn_tblock=SAMPLE_PAD, in_bb=16, in_tblock=SAMPLE_PAD,
                  act_dtype=F32)


def kernel(x_prompt, x_sample, state_ssm_re, state_ssm_im, state_hgrn, state_wkv, state_shift, norm_w, final_norm_w, even_w_in, even_w_out, ssm_lambda_re, ssm_lambda_im, ssm_log_step, ssm_b_re, ssm_b_im, ssm_c_re, ssm_c_im, ssm_d, ssm_glu_w, ssm_glu_b, hgrn_lower_bounds, hgrn_norm_w, rw_mix, rw_w_rkvz, rw_w0, rw_w1, rw_w2, rw_a0, rw_a1, rw_a2, rw_v0, rw_v1, rw_v2, rw_g1, rw_g2, rw_k_k, rw_k_a, rw_r_k, rw_ln_w, rw_ln_b, rw_w_o):
    w = dict(norm_w=norm_w, final_norm_w=final_norm_w, even_w_in=even_w_in, even_w_out=even_w_out,
             ssm_lambda_re=ssm_lambda_re, ssm_lambda_im=ssm_lambda_im, ssm_log_step=ssm_log_step,
             ssm_b_re=ssm_b_re, ssm_b_im=ssm_b_im, ssm_c_re=ssm_c_re, ssm_c_im=ssm_c_im, ssm_d=ssm_d,
             ssm_glu_w=ssm_glu_w, ssm_glu_b=ssm_glu_b, hgrn_lower_bounds=hgrn_lower_bounds,
             hgrn_norm_w=hgrn_norm_w, rw_mix=rw_mix, rw_w_rkvz=rw_w_rkvz, rw_w0=rw_w0, rw_w1=rw_w1,
             rw_w2=rw_w2, rw_a0=rw_a0, rw_a1=rw_a1, rw_a2=rw_a2, rw_v0=rw_v0, rw_v1=rw_v1, rw_v2=rw_v2,
             rw_g1=rw_g1, rw_g2=rw_g2, rw_k_k=rw_k_k, rw_k_a=rw_k_a, rw_r_k=rw_r_k,
             rw_ln_w=rw_ln_w, rw_ln_b=rw_ln_b, rw_w_o=rw_w_o)
    for name in MATMUL_WEIGHTS:
        w[name] = w[name].astype(BF16)
    n_even, n_odd = state_hgrn.shape[0], state_wkv.shape[0]
    lbs = jax.nn.softmax(hgrn_lower_bounds, axis=0)
    w_in = w["even_w_in"]
    f_lo, f_hi = 2 * A_WIDTH + B_WIDTH, 2 * A_WIDTH + 2 * B_WIDTH
    prep = dict(lbs=jnp.cumsum(lbs, axis=0) - lbs[0], s5=[_s5_params(w, j) for j in range(n_even)],
                w_in=jnp.concatenate([w_in[..., :f_lo], w_in[..., f_hi:], w_in[..., f_lo:f_hi]], axis=-1))
    bp, bs, ts = x_prompt.shape[0], x_sample.shape[0], x_sample.shape[1]
    assert ts == SAMPLE_CFG["t_valid"]
    z_re = jnp.zeros((n_even, bp, A_GROUPS, A_STATE), F32)
    z_hg = jnp.zeros((n_even, bp, B_HEADS, B_DIM, B_DIM), F32)
    z_wkv = jnp.zeros((n_odd, bp, C_HEADS, C_HEAD, C_HEAD), F32)
    z_sh = jnp.zeros((n_odd, bp, D_MODEL), F32)
    prompt = _trunk(x_prompt, z_re, z_re, z_hg, z_wkv, z_sh, w, prep, PROMPT_CFG)
    xs = jnp.pad(x_sample, ((0, 0), (0, SAMPLE_PAD - ts), (0, 0)))
    sample = _trunk(xs, state_ssm_re, state_ssm_im, state_hgrn, state_wkv, state_shift, w, prep, SAMPLE_CFG)
    y_sample = sample[0][:, :ts]
    return (prompt[0], y_sample, *prompt[1:], *sample[1:])
```

```python
import functools
import math

import jax
import jax.numpy as jnp
from jax import lax
from jax.experimental import pallas as pl
from jax.experimental.pallas import tpu as pltpu

F32 = jnp.float32
BF16 = jnp.bfloat16

D_MODEL = 1024
A_WIDTH = 512
A_GROUP = 16
A_GROUPS = 32
A_STATE = 64
A_LANES = A_GROUPS * A_STATE
A_BLOCKS = 4
B_WIDTH = 512
PROJ_U, PROJ_ZA, PROJ_Q, PROJ_I, PROJ_ZB = range(5)
B_HEADS = 4
B_DIM = 128
C_HEADS = 16
C_HEAD = 64
C_PACK = 4
SEG_ROWS = 128
DECAY_SCALE = math.exp(-0.5)
RMS_EPS = 1e-6
GN_EPS = 64e-5
VMEM_LIMIT = 56 * 1024 * 1024
LANES = 128


def _dot(a, b):
    return jnp.dot(a.astype(b.dtype), b, preferred_element_type=F32)


def _bdot(a, b, contract):
    return lax.dot_general(a, b, (contract, ((0,), (0,))), preferred_element_type=F32)


def _nn(a, b):
    return _bdot(a, b, ((2,), (1,)))


def _nt(a, b):
    return _bdot(a, b, ((2,), (2,)))


def _tn(a, b):
    return _bdot(a, b, ((1,), (1,)))


def _rms(x, w):
    return x * lax.rsqrt(jnp.mean(x * x, axis=-1, keepdims=True) + RMS_EPS) * w


def _sigmoid(x):
    return 1.0 / (1.0 + jnp.exp(-x))


def _silu(x):
    return x * _sigmoid(x)


def _params(sem):
    return pltpu.CompilerParams(dimension_semantics=sem, vmem_limit_bytes=VMEM_LIMIT)


def _stacked_state_specs(prev, state_shape, n_layers, layer, bb):
    tail = (0,) * (len(state_shape) - 1)
    shape = jax.ShapeDtypeStruct((n_layers, *state_shape), F32)
    block = (bb, *state_shape[1:])
    if prev is None:
        return pl.BlockSpec((n_layers, *block), lambda i, j: (0, i, *tail)), shape, layer, False
    return pl.BlockSpec((1, *block), lambda i, j: (layer, i, *tail)), shape, 0, True


def _write_state_slab(ref, slot, aliased, value):
    if not aliased:
        for other in range(ref.shape[0]):
            if other != slot:
                ref[other] = jnp.zeros(ref.shape[1:], F32)
    ref[slot] = value


def _tri_masks(c):
    row = lax.broadcasted_iota(jnp.int32, (c, c), 0)
    col = lax.broadcasted_iota(jnp.int32, (c, c), 1)
    return row >= col, row > col, row == col


def _split_heads(x, width):
    bb, _, d = x.shape
    return jnp.stack([x[i, :, h * width:(h + 1) * width] for i in range(bb) for h in range(d // width)], axis=0)


def _merge_heads(x, bb):
    h = x.shape[0] // bb
    return jnp.stack([jnp.concatenate([x[i * h + j] for j in range(h)], axis=-1) for i in range(bb)], axis=0)


def _cumsum_time(tri_f, x):
    hi = x.astype(BF16).astype(F32)
    rest = x - hi
    mid = rest.astype(BF16).astype(F32)
    lo = rest - mid
    return jnp.stack([sum(jnp.dot(tri_f, p[i], preferred_element_type=F32) for p in (hi, mid, lo))
                      for i in range(x.shape[0])], axis=0)


def _norm_matmul_kernel(x_ref, nw_ref, w_ref, rest_ref, f_ref):
    res = _dot(_rms(x_ref[...], nw_ref[...]), w_ref[...])
    n_rest = rest_ref.shape[-1]
    rest_ref[...] = res[:, :n_rest].astype(rest_ref.dtype)
    f_ref[...] = res[:, n_rest:]


def _norm_matmul(x, nw, w, *, tm, n_f32, act_dtype):
    m, k = x.shape
    n = w.shape[1]
    row = lambda width: pl.BlockSpec((tm, width), lambda i: (i, 0))
    return pl.pallas_call(
        _norm_matmul_kernel, grid=(m // tm,),
        in_specs=[row(k), pl.BlockSpec((1, k), lambda i: (0, 0)), pl.BlockSpec((k, n), lambda i: (0, 0))],
        out_specs=[row(n - n_f32), row(n_f32)],
        out_shape=[jax.ShapeDtypeStruct((m, n - n_f32), act_dtype), jax.ShapeDtypeStruct((m, n_f32), F32)],
        compiler_params=_params(("parallel",)), name="norm_in_proj",
    )(x, nw, w)


def _proj_residual_kernel(*refs, n_x, final_norm):
    h_ref, x_refs, w_ref = refs[0], refs[1:1 + n_x], refs[1 + n_x]
    x = jnp.concatenate([r[...] for r in x_refs], axis=-1) if n_x > 1 else x_refs[0][...]
    h = h_ref[...] + _dot(x, w_ref[...])
    if final_norm:
        nw_ref, h_out, y_out = refs[2 + n_x:]
        y_out[...] = _rms(h, nw_ref[...])
    else:
        h_out = refs[2 + n_x]
    h_out[...] = h


def _proj_residual(h, xs, w, final_nw=None, *, tm):
    m, n = h.shape
    final_norm = final_nw is not None
    row = lambda width: pl.BlockSpec((tm, width), lambda i: (i, 0))
    in_specs = [row(n)] + [row(x.shape[1]) for x in xs] + [pl.BlockSpec(w.shape, lambda i: (0, 0))]
    args = [h, *xs, w]
    out_specs, out_shape = [row(n)], [jax.ShapeDtypeStruct((m, n), F32)]
    if final_norm:
        in_specs.append(pl.BlockSpec((1, n), lambda i: (0, 0)))
        args.append(final_nw)
        out_specs.append(row(n))
        out_shape.append(jax.ShapeDtypeStruct((m, n), F32))
    return pl.pallas_call(
        functools.partial(_proj_residual_kernel, n_x=len(xs), final_norm=final_norm), grid=(m // tm,),
        in_specs=in_specs, out_specs=out_specs, out_shape=out_shape,
        compiler_params=_params(("parallel",)), name="out_proj_residual",
    )(*args)


def _s5_kernel(u_ref, za_ref, h0r_ref, h0i_ref, ar_ref, ai_ref, bb_ref, c_ref,
               d_ref, gw_ref, gb_ref, out_ref, hfr_ref, hfi_ref, hre, him, bur, bui, st_r, st_i, wbm, wtm,
               *, bsz, tc, n_steps):
    i = pl.program_id(1)
    rows = bsz * tc
    n_tiles = A_LANES // LANES
    u_tiles = A_WIDTH // LANES
    per = n_tiles // A_BLOCKS
    assert A_WIDTH // A_BLOCKS == LANES

    @pl.when(i == 0)
    def _():
        for cb in range(n_tiles):
            st_r[cb] = h0r_ref[:, cb * LANES:(cb + 1) * LANES]
            st_i[cb] = h0i_ref[:, cb * LANES:(cb + 1) * LANES]

    def bm(t):
        return pl.ds(t, bsz, stride=tc)

    def tm(t):
        return pl.ds(pl.multiple_of(t * bsz, bsz), bsz)

    def reorder(src, dst, src_rows, dst_rows):
        def body(t, carry):
            for q in range(u_tiles):
                dst.at[q][dst_rows(t), :] = src.at[q][src_rows(t), :]
            return carry
        lax.fori_loop(0, tc, body, 0, unroll=min(tc, 8))

    u_bm = u_ref[...].astype(F32).reshape(rows, A_WIDTH)
    for q in range(u_tiles):
        wbm[q] = u_bm[:, q * LANES:(q + 1) * LANES]
    reorder(wbm, wtm, bm, tm)

    for j in range(A_BLOCKS):
        bu = _dot(wtm[j], bb_ref[j])
        for q in range(per):
            bur[j * per + q] = bu[:, q * LANES:(q + 1) * LANES]
            bui[j * per + q] = bu[:, (per + q) * LANES:(per + q + 1) * LANES]

    def advance(t, prev):
        new = []
        for cb in range(n_tiles):
            lanes = slice(cb * LANES, (cb + 1) * LANES)
            ar, ai = ar_ref[:, lanes], ai_ref[:, lanes]
            pr, pi = prev[2 * cb], prev[2 * cb + 1]
            hr = ar * pr - ai * pi + bur[cb, tm(t), :]
            hi = ar * pi + ai * pr + bui[cb, tm(t), :]
            hre[cb, tm(t), :] = hr
            him[cb, tm(t), :] = hi
            new += [hr, hi]
        return tuple(new)

    state = tuple(ref[cb] for cb in range(n_tiles) for ref in (st_r, st_i))
    if n_steps <= 8:
        for t in range(n_steps):
            state = advance(t, state)
    else:
        state = lax.fori_loop(0, n_steps, advance, state)
    for cb in range(n_tiles):
        st_r[cb] = state[2 * cb]
        st_i[cb] = state[2 * cb + 1]
    if n_steps < tc:
        pad = pl.ds(n_steps * bsz, (tc - n_steps) * bsz)
        for cb in range(n_tiles):
            hre[cb, pad, :] = bur[cb, pad, :]
            him[cb, pad, :] = bui[cb, pad, :]

    def block(j):
        return jnp.concatenate([ref[j * per + q] for ref in (hre, him) for q in range(per)], axis=-1)

    u_tm = jnp.concatenate([wtm[q] for q in range(u_tiles)], axis=-1)
    y = jnp.concatenate([_dot(block(j), c_ref[j]) for j in range(A_BLOCKS)], axis=-1) + d_ref[...] * u_tm
    y = 0.5 * y * (1.0 + jnp.tanh(math.sqrt(2.0 / math.pi) * (y + 0.044715 * (y * y * y))))
    y = y * _sigmoid(_dot(y, gw_ref[...]) + gb_ref[...])
    for q in range(u_tiles):
        wtm[q] = y[:, q * LANES:(q + 1) * LANES]
    reorder(wtm, wbm, tm, bm)
    y_bm = jnp.concatenate([wbm[q] for q in range(u_tiles)], axis=-1)
    out = y_bm * _silu(za_ref[...].astype(F32).reshape(rows, A_WIDTH))
    out_ref[...] = out.reshape(bsz, tc, A_WIDTH).astype(out_ref.dtype)

    @pl.when(i == pl.num_programs(1) - 1)
    def _():
        for cb in range(n_tiles):
            hfr_ref[:, cb * LANES:(cb + 1) * LANES] = st_r[cb]
            hfi_ref[:, cb * LANES:(cb + 1) * LANES] = st_i[cb]


def _s5_mixer(proj, h0r, h0i, sp, *, bb, tc, t_valid):
    bsz, t, _ = proj.shape
    n_steps = tc if t_valid is None else t_valid
    full = lambda a: pl.BlockSpec(a.shape, lambda b, i: (0,) * a.ndim)
    col = lambda c: pl.BlockSpec((bb, tc, A_WIDTH), lambda b, i: (b, i, c))
    st = pl.BlockSpec((bb, A_LANES), lambda b, i: (b, 0))
    assert (PROJ_U, PROJ_ZA) == (0, 1)
    weights = [sp["ab_re"], sp["ab_im"], sp["bb"], sp["c"], sp["d"], sp["glu_w"], sp["glu_b"]]
    state = jax.ShapeDtypeStruct((bsz, A_LANES), F32)
    return pl.pallas_call(
        functools.partial(_s5_kernel, bsz=bb, tc=tc, n_steps=n_steps), grid=(bsz // bb, t // tc),
        in_specs=[col(0), col(1), st, st] + [full(a) for a in weights],
        out_specs=[col(0), st, st],
        out_shape=[jax.ShapeDtypeStruct((bsz, t, A_WIDTH), proj.dtype), state, state],
        scratch_shapes=[pltpu.VMEM((A_LANES // LANES, bb * tc, LANES), F32)] * 4
        + [pltpu.VMEM((A_LANES // LANES, bb, LANES), F32)] * 2
        + [pltpu.VMEM((A_WIDTH // LANES, bb * tc, LANES), F32)] * 2,
        compiler_params=_params(("parallel", "arbitrary")), name="s5_mixer",
    )(proj, proj, h0r, h0i, *weights)


def _hgrn_kernel(*refs, bb, chunk, n_chunks, t_valid, use_lb, slot, aliased):
    q_ref, f_ref, i_ref, zb_ref, lb_ref, nw_ref, s0_ref = refs[:7]
    out_ref, sfin_ref, s_scr = refs[-3:]
    c = chunk
    tb = pl.program_id(1)

    @pl.when(tb == 0)
    def _():
        s_scr[...] = jnp.swapaxes(s0_ref[0].reshape(bb * B_HEADS, B_DIM, B_DIM), 1, 2)

    incl, _, _ = _tri_masks(c)
    tri_f = incl.astype(F32)

    def body(ci, carry):
        off = pl.multiple_of(ci * c, c)
        sl = pl.ds(off, c)
        q = _silu(q_ref[:, sl, :].astype(F32))
        f = f_ref[:, sl, :]
        e = jnp.exp(-jnp.abs(f))
        r = 1.0 / (1.0 + e)
        k = jnp.where(f >= 0.0, e * r, r)
        if use_lb:
            lb = lb_ref[...]
            log_f = jnp.log(lb + (1.0 - lb) * jnp.where(f >= 0.0, r, e * r))
            k = (1.0 - lb) * k
        else:
            log_f = jnp.minimum(f, 0.0) - jnp.log1p(e)
        if t_valid is not None:
            tok = tb * (n_chunks * c) + off + lax.broadcasted_iota(jnp.int32, (1, c, 1), 1)
            log_f = jnp.where(tok < t_valid, log_f, 0.0)
            k = jnp.where(tok < t_valid, k, 0.0)
        cum = _cumsum_time(tri_f, log_f)
        last = cum[:, c - 1:c, :]
        mid = cum[:, c // 2 - 1:c // 2, :]
        rel = jnp.exp(cum - mid)
        q_mid = q * rel
        k_mid = k * (1.0 / rel)
        qm = _split_heads(q_mid, B_DIM)
        kh = _split_heads(k_mid, B_DIM)
        qh = _split_heads(q_mid * jnp.exp(mid), B_DIM)
        ke = _split_heads(k_mid * jnp.exp(last - mid), B_DIM)
        vh = _split_heads(i_ref[:, sl, :].astype(F32), B_DIM)
        gl = _split_heads(jnp.exp(last), B_DIM)
        s = s_scr[...]
        att = jnp.where(incl[None], _nt(qm, kh), 0.0)
        o = _nn(att, vh) + _nt(qh, s)
        s_scr[...] = s * gl + _tn(vh, ke)
        o = o * lax.rsqrt(jnp.mean(o * o, axis=-1, keepdims=True) + RMS_EPS) * nw_ref[...]
        out_ref[:, sl, :] = (_merge_heads(o, bb) * _silu(zb_ref[:, sl, :].astype(F32))).astype(out_ref.dtype)
        return carry

    lax.fori_loop(0, n_chunks, body, 0)

    @pl.when(tb == pl.num_programs(1) - 1)
    def _():
        _write_state_slab(sfin_ref, slot, aliased,
                          jnp.swapaxes(s_scr[...], 1, 2).reshape(bb, B_HEADS, B_DIM, B_DIM))


def _hgrn_mixer(proj, proj_f, s0, lb, nw, prev, *, layer, n_layers, bb, tblock, chunk, t_valid, use_lb):
    bsz, t, _ = proj.shape
    st_out, st_shape, slot, aliased = _stacked_state_specs(prev, s0.shape[1:], n_layers, layer, bb)
    extra = ([pl.BlockSpec(memory_space=pl.ANY)], [prev], {7: 1}) if aliased else ([], [], {})
    col = lambda c: pl.BlockSpec((bb, tblock, B_WIDTH), lambda i, j: (i, j, c))
    vec = lambda a: pl.BlockSpec(a.shape, lambda i, j: (0, 0))
    st = pl.BlockSpec((1, bb, B_HEADS, B_DIM, B_DIM), lambda i, j: (layer, i, 0, 0, 0))
    return pl.pallas_call(
        functools.partial(_hgrn_kernel, bb=bb, chunk=chunk, n_chunks=tblock // chunk, t_valid=t_valid,
                          use_lb=use_lb, slot=slot, aliased=aliased),
        grid=(bsz // bb, t // tblock),
        in_specs=[col(PROJ_Q), col(0), col(PROJ_I), col(PROJ_ZB), vec(lb), vec(nw), st] + extra[0],
        out_specs=[pl.BlockSpec((bb, tblock, B_WIDTH), lambda i, j: (i, j, 0)), st_out],
        out_shape=[jax.ShapeDtypeStruct((bsz, t, B_WIDTH), proj.dtype), st_shape],
        scratch_shapes=[pltpu.VMEM((bb * B_HEADS, B_DIM, B_DIM), F32)],
        input_output_aliases=extra[2],
        compiler_params=_params(("parallel", "arbitrary")), name="hgrn2_mixer",
    )(proj, proj_f, proj, proj, lb, nw, s0, *extra[1])


def _rwkv_in_kernel(*refs, tblock, last_row, vres):
    h_ref, nw_ref, sh_ref, mix_ref, wp_ref, w0, w1, w2, a0, a1, a2, g1, g2 = refs[:13]
    if vres:
        v0, v1, v2, vf_ref = refs[13:17]
    r_ref, k_ref, v_ref, gate_ref, lw_ref, al_ref, last_ref, carry = refs[-8:]

    @pl.when(pl.program_id(1) == 0)
    def _():
        carry[...] = sh_ref[...]

    xn = _rms(h_ref[...], nw_ref[...])
    row = lax.broadcasted_iota(jnp.int32, (1, tblock, 1), 1)
    xx = jnp.where(row == 0, carry[...], pltpu.roll(xn, 1, axis=1)) - xn
    carry[...] = xn[:, tblock - 1:tblock, :]
    last_ref[...] = xn[:, last_row:last_row + 1, :]
    shape = xn.shape
    xn = xn.reshape(-1, shape[-1])
    xx = xx.reshape(-1, shape[-1])
    mixed = lambda m: xn + xx * mix_ref[m:m + 1, :]
    store = lambda ref, val: ref.__setitem__(Ellipsis, val.reshape(shape).astype(ref.dtype))
    x_v, x_g = mixed(3), mixed(5)
    store(r_ref, _dot(mixed(0), wp_ref[0]))
    store(k_ref, _dot(mixed(2), wp_ref[1]))
    v = _dot(x_v, wp_ref[2])
    if vres:
        v_first = vf_ref[...].reshape(v.shape).astype(F32)
        v = v + (v_first - v) * _sigmoid(v0[...] + _dot(_dot(x_v, v1[...]), v2[...]))
    store(v_ref, v)
    store(gate_ref, _dot(_sigmoid(_dot(x_g, g1[...])), g2[...]) * _silu(_dot(x_g, wp_ref[3])))
    store(lw_ref, -DECAY_SCALE * _sigmoid(w0[...] + _dot(jnp.tanh(_dot(mixed(1), w1[...])), w2[...])))
    store(al_ref, _sigmoid(a0[...] + _dot(_dot(mixed(4), a1[...]), a2[...])))


def _rwkv_in(h, nw, shift, mix, wp, lora_w, v_first, *, bb, tblock, t_valid, act_dtype):
    bsz, t, d = h.shape
    vres = v_first is not None
    tok = pl.BlockSpec((bb, tblock, d), lambda i, j: (i, j, 0))
    one = pl.BlockSpec((bb, 1, d), lambda i, j: (i, 0, 0))
    full = lambda a: pl.BlockSpec(a.shape, lambda i, j: (0,) * a.ndim)
    in_specs = [tok, full(nw), one, full(mix), full(wp)] + [full(a) for a in lora_w]
    args = [h, nw, shift.reshape(bsz, 1, d), mix, wp, *lora_w]
    if vres:
        in_specs.append(tok)
        args.append(v_first)
    act = lambda dt: jax.ShapeDtypeStruct((bsz, t, d), dt)
    last_row = (tblock if t_valid is None else t_valid) - 1
    return pl.pallas_call(
        functools.partial(_rwkv_in_kernel, tblock=tblock, last_row=last_row, vres=vres),
        grid=(bsz // bb, t // tblock),
        in_specs=in_specs, out_specs=[tok] * 6 + [one],
        out_shape=[act(act_dtype)] * 4 + [act(F32)] * 2 + [jax.ShapeDtypeStruct((bsz, 1, d), F32)],
        scratch_shapes=[pltpu.VMEM((bb, 1, d), F32)],
        compiler_params=_params(("parallel", "arbitrary")), name="rwkv_in_proj",
    )(*args)


def _rwkv_kernel(*refs, bb, chunk, n_chunks, t_valid, slot, aliased):
    r_ref, k_ref, v_ref, gate_ref, lw_ref, al_ref, kk_ref, ka_ref, rk_ref, lnw_ref, lnb_ref, seg_ref, s0_ref = refs[:13]
    y_ref, sfin_ref, s_scr = refs[-3:]
    c = chunk
    gw = C_PACK * C_HEAD
    aw = C_PACK * c
    n_groups = D_MODEL // gw
    n_sq = int(math.log2(c)) - 1
    tb = pl.program_id(1)

    @pl.when(tb == 0)
    def _():
        for b in range(bb):
            s_scr[b] = jnp.concatenate([s0_ref[b, h] for h in range(C_HEADS)], axis=-1)

    row = lax.broadcasted_iota(jnp.int32, (c, aw), 0)
    col = lax.broadcasted_iota(jnp.int32, (c, aw), 1) % c
    incl, strict = row >= col, row > col
    eye = (row == col).astype(F32)
    tri_f = _tri_masks(c)[0].astype(F32)

    def bd_mask(rows_per_head, lanes, lanes_per_head):
        r = lax.broadcasted_iota(jnp.int32, (C_PACK * rows_per_head, lanes), 0) // rows_per_head
        l = lax.broadcasted_iota(jnp.int32, (C_PACK * rows_per_head, lanes), 1) // lanes_per_head
        return r == l

    bd_c, bd_v, bd_p = bd_mask(c, gw, C_HEAD), bd_mask(C_HEAD, gw, C_HEAD), bd_mask(c, aw, c)
    lane_head = lax.broadcasted_iota(jnp.int32, (1, gw), 1) // C_HEAD
    seg = seg_ref[...]

    def bd(y, mask):
        return jnp.where(mask, jnp.tile(y, (C_PACK, 1)), 0.0)

    def seg_sum(xs, pieces=1):
        rows = xs[0].shape[0]
        per_dot = max(1, SEG_ROWS // (rows * pieces))
        result = []
        for start in range(0, len(xs), per_dot):
            parts = []
            for x in xs[start:start + per_dot]:
                for _ in range(pieces):
                    parts.append(x.astype(BF16).astype(F32))
                    x = x - parts[-1]
            out = jnp.dot(jnp.concatenate(parts, axis=0), seg, preferred_element_type=F32)
            blocks = [out[i * rows:(i + 1) * rows] for i in range(len(parts))]
            result += [sum(blocks[i:i + pieces]) for i in range(0, len(parts), pieces)]
        return result

    def nn(a, b):
        return jnp.dot(a, b, preferred_element_type=F32)

    def nt(a, b):
        return lax.dot_general(a, b, (((1,), (1,)), ((), ())), preferred_element_type=F32)

    def tn(a, b):
        return lax.dot_general(a, b, (((0,), (0,)), ((), ())), preferred_element_type=F32)

    def cat(a, b):
        return jnp.concatenate([a, b], axis=0)

    def body(ci, carry):
        off = pl.multiple_of(ci * c, c)
        sl = pl.ds(off, c)
        inst = [(b, slice(g * gw, (g + 1) * gw)) for b in range(bb) for g in range(n_groups)]
        r_l, k_l, v_l, kk_l, al_l, rk_l, gi_l, gv_l, gm_l = ([] for _ in range(9))
        for b in range(bb):
            r = r_ref[b, sl, :].astype(F32)
            k = k_ref[b, sl, :].astype(F32)
            v = v_ref[b, sl, :].astype(F32)
            lw = lw_ref[b, sl, :]
            al = al_ref[b, sl, :]
            kk_raw = k * kk_ref[...]
            k = k * (1.0 + (al - 1.0) * ka_ref[...])
            rk = r * k * rk_ref[...]
            if t_valid is not None:
                ok = tb * (n_chunks * c) + off + lax.broadcasted_iota(jnp.int32, (c, 1), 0) < t_valid
                lw = jnp.where(ok, lw, 0.0)
                k = jnp.where(ok, k, 0.0)
                al = jnp.where(ok, al, 0.0)
            cum = _cumsum_time(tri_f, lw[None])[0]
            g_in, g_inv, g_m = jnp.exp(cum), jnp.exp(-cum), jnp.exp(cum - lw)
            for g in range(n_groups):
                ln = slice(g * gw, (g + 1) * gw)
                for lst, val in ((r_l, r), (k_l, k), (v_l, v), (kk_l, kk_raw), (al_l, al), (rk_l, rk),
                                 (gi_l, g_in), (gv_l, g_inv), (gm_l, g_m)):
                    lst.append(val[:, ln])
        n = len(inst)
        each = range(n)
        ss_l = seg_sum([kk_l[i] * kk_l[i] for i in each], pieces=2)
        kk_l = [kk_l[i] / jnp.maximum(jnp.sqrt(ss_l[i]), 1e-12) for i in each]
        kt_l = [k_l[i] * gv_l[i] for i in each]
        bt_l = [kk_l[i] * al_l[i] * gv_l[i] for i in each]
        x_l = [cat(-kk_l[i] * gm_l[i], r_l[i] * gi_l[i]) for i in each]
        s_l = [s_scr[b, :, ln] for b, ln in inst]
        a_all = [nt(x_l[i], jnp.concatenate([bd(bt_l[i], bd_c), bd(kt_l[i], bd_c), bd(s_l[i], bd_v)], axis=0))
                 for i in each]
        a_b = [a[:, :aw] for a in a_all]
        a_k = [a[:, aw:2 * aw] for a in a_all]
        x_s = [a[:, 2 * aw:] for a in a_all]
        n_ab = [jnp.where(strict, a_b[i][:c], 0.0) for i in each]
        a_rb = [jnp.where(incl, a_b[i][c:], 0.0) for i in each]
        a_kk = [cat(jnp.where(strict, a_k[i][:c], 0.0), jnp.where(incl, a_k[i][c:], 0.0)) for i in each]
        a_v = [nn(a_kk[i], bd(v_l[i], bd_c)) for i in each]
        w_m = [x_s[i][:c] + a_v[i][:c] for i in each]
        t_m = [eye + n_ab[i] for i in each]
        p = n_ab
        if n_sq >= 1:
            p = [nn(p[i], bd(p[i], bd_p)) for i in each]
            for _ in range(n_sq - 1):
                both = [nn(cat(p[i], t_m[i]), bd(p[i], bd_p)) for i in each]
                p = [both[i][:c] for i in each]
                t_m = [t_m[i] + both[i][c:] for i in each]
            t_m = [t_m[i] + nn(t_m[i], bd(p[i], bd_p)) for i in each]
        u = [nn(t_m[i], bd(w_m[i], bd_c)) for i in each]
        y = [x_s[i][c:] + a_v[i][c:] for i in each]
        y_u = [nn(a_rb[i], bd(u[i], bd_c)) for i in each]
        full = [tn(cat(v_l[i], u[i]), cat(kt_l[i], bt_l[i])) for i in each]
        for i, (b, ln) in enumerate(inst):
            upd = sum(jnp.where(lane_head == h, full[i][h * C_HEAD:(h + 1) * C_HEAD], 0.0) for h in range(C_PACK))
            s_scr[b, :, ln] = (s_l[i] + upd) * gi_l[i][c - 1:c]
        y = [y[i] + y_u[i] for i in each]
        sums = seg_sum([cat(y[i], rk_l[i]) for i in each])
        dev = [y[i] - sums[i][:c] * (1.0 / C_HEAD) for i in each]
        var = seg_sum([dev[i] * dev[i] for i in each])
        bonus = [sums[i][c:] * v_l[i] for i in each]
        yn = [dev[i] * lax.rsqrt(var[i] * (1.0 / C_HEAD) + GN_EPS) for i in each]
        for b in range(bb):
            mine = range(b * n_groups, (b + 1) * n_groups)
            yn_b = jnp.concatenate([yn[i] for i in mine], axis=-1)
            bonus_b = jnp.concatenate([bonus[i] for i in mine], axis=-1)
            y_ref[b, sl, :] = ((yn_b * lnw_ref[...] + lnb_ref[...] + bonus_b)
                               * gate_ref[b, sl, :].astype(F32)).astype(y_ref.dtype)
        return carry

    lax.fori_loop(0, n_chunks, body, 0)

    @pl.when(tb == pl.num_programs(1) - 1)
    def _():
        final = jnp.stack([jnp.stack([s_scr[b, :, h * C_HEAD:(h + 1) * C_HEAD] for h in range(C_HEADS)])
                           for b in range(bb)])
        _write_state_slab(sfin_ref, slot, aliased, final)


def _rwkv_mixer(r, k, v, gate, lw, al, vecs, s0, prev, *, layer, n_layers, bb, tblock, chunk, t_valid):
    bsz, t, d = r.shape
    st_out, st_shape, slot, aliased = _stacked_state_specs(prev, s0.shape[1:], n_layers, layer, bb)
    tok = pl.BlockSpec((bb, tblock, d), lambda i, j: (i, j, 0))
    vec = pl.BlockSpec((1, d), lambda i, j: (0, 0))
    nb = bsz // bb
    st = pl.BlockSpec((bb, C_HEADS, C_HEAD, C_HEAD), lambda i, j: (layer * nb + i, 0, 0, 0))
    s0 = s0.reshape(-1, *s0.shape[2:])
    seg = jnp.kron(jnp.eye(C_PACK, dtype=F32), jnp.ones((C_HEAD, C_HEAD), F32))
    in_specs = [tok] * 6 + [vec] * 5 + [pl.BlockSpec(seg.shape, lambda i, j: (0, 0)), st]
    args = [r, k, v, gate, lw, al, *vecs, seg, s0]
    aliases = {}
    if aliased:
        aliases = {len(args): 1}
        in_specs.append(pl.BlockSpec(memory_space=pl.ANY))
        args.append(prev)
    return pl.pallas_call(
        functools.partial(_rwkv_kernel, bb=bb, chunk=chunk, n_chunks=tblock // chunk, t_valid=t_valid,
                          slot=slot, aliased=aliased),
        grid=(bsz // bb, t // tblock),
        in_specs=in_specs, out_specs=[tok, st_out], out_shape=[jax.ShapeDtypeStruct((bsz, t, d), r.dtype), st_shape],
        input_output_aliases=aliases,
        scratch_shapes=[pltpu.VMEM((bb, C_HEAD, d), F32)],
        compiler_params=_params(("parallel", "arbitrary")), name="rwkv7_mixer",
    )(*args)


def _s5_params(w, j):
    lr = jnp.minimum(w["ssm_lambda_re"][j], -1e-4)
    li = w["ssm_lambda_im"][j]
    step = jnp.exp(w["ssm_log_step"][j])[:, None]
    mag = jnp.exp(lr * step)
    ab_re = mag * jnp.cos(li * step)
    ab_im = mag * jnp.sin(li * step)
    den = lr * lr + li * li
    nr = ab_re - 1.0
    cr = (nr * lr + ab_im * li) / den
    ci = (ab_im * lr - nr * li) / den
    b_re, b_im = w["ssm_b_re"][j], w["ssm_b_im"][j]
    bb_re = cr[..., None] * b_re - ci[..., None] * b_im
    bb_im = cr[..., None] * b_im + ci[..., None] * b_re
    gpb = A_GROUPS // A_BLOCKS
    eye = jnp.eye(gpb, dtype=F32)

    def pack_in(bb):
        bb = bb.reshape(A_BLOCKS, gpb, A_STATE, A_GROUP)
        return jnp.einsum("bgph,gk->bghkp", bb, eye).reshape(A_BLOCKS, gpb * A_GROUP, gpb * A_STATE)

    def pack_out(c):
        c = c.reshape(A_BLOCKS, gpb, A_GROUP, A_STATE)
        return jnp.einsum("bghp,gk->bgpkh", c, eye).reshape(A_BLOCKS, gpb * A_STATE, gpb * A_GROUP)

    return dict(ab_re=ab_re.reshape(1, A_LANES), ab_im=ab_im.reshape(1, A_LANES),
                bb=jnp.concatenate([pack_in(bb_re), pack_in(bb_im)], axis=-1).astype(BF16),
                c=jnp.concatenate([pack_out(w["ssm_c_re"][j]), -pack_out(w["ssm_c_im"][j])], axis=1).astype(BF16),
                d=w["ssm_d"][j].reshape(1, A_WIDTH), glu_w=w["ssm_glu_w"][j],
                glu_b=w["ssm_glu_b"][j].reshape(1, A_WIDTH))


def _trunk(x, ssm_re, ssm_im, hgrn, wkv, shift, w, prep, cfg):
    bsz, t, d = x.shape
    t_valid = cfg["t_valid"]
    m = bsz * t
    h = x.reshape(m, d)
    n_re, n_im, n_sh = [], [], []
    hg_all = wkv_all = None
    lbs = prep["lbs"]
    v_first = None
    y = None
    depth = w["norm_w"].shape[0]
    for layer in range(depth):
        j = layer // 2
        nw = w["norm_w"][layer].reshape(1, d)
        last = layer == depth - 1
        final_nw = w["final_norm_w"].reshape(1, d) if last else None
        if layer % 2 == 0:
            proj, proj_f = _norm_matmul(h, nw, prep["w_in"][j], tm=cfg["tm"] // 2, n_f32=B_WIDTH,
                                        act_dtype=cfg["act_dtype"])
            proj, proj_f = proj.reshape(bsz, t, -1), proj_f.reshape(bsz, t, -1)
            out_a, hr, hi = _s5_mixer(proj, ssm_re[j].reshape(bsz, A_LANES), ssm_im[j].reshape(bsz, A_LANES),
                                      prep["s5"][j], bb=cfg["s5_bb"], tc=cfg["s5_tc"], t_valid=t_valid)
            out_b, hg_all = _hgrn_mixer(proj, proj_f, hgrn, lbs[j].reshape(1, B_WIDTH),
                                        w["hgrn_norm_w"][j].reshape(1, B_DIM), hg_all, layer=j,
                                        n_layers=hgrn.shape[0], bb=cfg["hgrn_bb"], tblock=cfg["hgrn_tblock"],
                                        chunk=cfg["chunk"], t_valid=t_valid, use_lb=j > 0)
            res = _proj_residual(h, [out_a.reshape(m, A_WIDTH), out_b.reshape(m, B_WIDTH)], w["even_w_out"][j],
                                 final_nw, tm=cfg["tm"])
            n_re.append(hr.reshape(bsz, A_GROUPS, A_STATE))
            n_im.append(hi.reshape(bsz, A_GROUPS, A_STATE))
        else:
            row = lambda a: a.reshape(1, -1)
            lora_w = [row(w["rw_w0"][j]), w["rw_w1"][j], w["rw_w2"][j], row(w["rw_a0"][j]), w["rw_a1"][j],
                      w["rw_a2"][j], w["rw_g1"][j], w["rw_g2"][j]]
            if v_first is not None:
                lora_w += [row(w["rw_v0"][j - 1]), w["rw_v1"][j - 1], w["rw_v2"][j - 1]]
            r, k, v, gate, lw, al, x_last = _rwkv_in(h.reshape(bsz, t, d), nw, shift[j], w["rw_mix"][j],
                                                     w["rw_w_rkvz"][j], lora_w, v_first, bb=cfg["in_bb"],
                                                     tblock=cfg["in_tblock"], t_valid=t_valid,
                                                     act_dtype=cfg["act_dtype"])
            n_sh.append(x_last.reshape(bsz, d))
            if v_first is None:
                v_first = v
            vecs = [row(w[n][j]) for n in ("rw_k_k", "rw_k_a", "rw_r_k", "rw_ln_w", "rw_ln_b")]
            y_mix, wkv_all = _rwkv_mixer(r, k, v, gate, lw, al, vecs, wkv, wkv_all, layer=j,
                                         n_layers=wkv.shape[0], bb=cfg["bb"], tblock=cfg["tblock"],
                                         chunk=cfg["rwkv_chunk"], t_valid=t_valid)
            res = _proj_residual(h, [y_mix.reshape(m, d)], w["rw_w_o"][j], final_nw, tm=cfg["tm"])
        if last:
            h, y = res
        else:
            h = res[0]
    return (y.reshape(bsz, t, d), jnp.stack(n_re), jnp.stack(n_im), hg_all, wkv_all,
            jnp.stack(n_sh))


MATMUL_WEIGHTS = ("even_w_in", "even_w_out", "ssm_glu_w", "rw_w_rkvz", "rw_w1", "rw_w2", "rw_a1", "rw_a2",
                  "rw_v1", "rw_v2", "rw_g1", "rw_g2", "rw_w_o")
PROMPT_CFG = dict(t_valid=None, tm=1024, s5_bb=8, s5_tc=64, bb=2, tblock=128, chunk=32, rwkv_chunk=64,
                  hgrn_bb=8, hgrn_tblock=128, in_bb=1, in_tblock=512, act_dtype=BF16)
SAMPLE_PAD = 8
SAMPLE_CFG = dict(t_valid=4, tm=1024, s5_bb=32, s5_tc=SAMPLE_PAD, bb=8, tblock=SAMPLE_PAD, chunk=SAMPLE_PAD,
                  rwkv_chunk=SAMPLE_PAD, hgrn_bb=8, hgrn_tblock=SAMPLE_PAD, in_bb=32, in_tblock=SAMPLE_PAD,
                  act_dtype=F32)


def kernel(x_prompt, x_sample, state_ssm_re, state_ssm_im, state_hgrn, state_wkv, state_shift, norm_w, final_norm_w, even_w_in, even_w_out, ssm_lambda_re, ssm_lambda_im, ssm_log_step, ssm_b_re, ssm_b_im, ssm_c_re, ssm_c_im, ssm_d, ssm_glu_w, ssm_glu_b, hgrn_lower_bounds, hgrn_norm_w, rw_mix, rw_w_rkvz, rw_w0, rw_w1, rw_w2, rw_a0, rw_a1, rw_a2, rw_v0, rw_v1, rw_v2, rw_g1, rw_g2, rw_k_k, rw_k_a, rw_r_k, rw_ln_w, rw_ln_b, rw_w_o):
    w = dict(norm_w=norm_w, final_norm_w=final_norm_w, even_w_in=even_w_in, even_w_out=even_w_out,
             ssm_lambda_re=ssm_lambda_re, ssm_lambda_im=ssm_lambda_im, ssm_log_step=ssm_log_step,
             ssm_b_re=ssm_b_re, ssm_b_im=ssm_b_im, ssm_c_re=ssm_c_re, ssm_c_im=ssm_c_im, ssm_d=ssm_d,
             ssm_glu_w=ssm_glu_w, ssm_glu_b=ssm_glu_b, hgrn_lower_bounds=hgrn_lower_bounds,
             hgrn_norm_w=hgrn_norm_w, rw_mix=rw_mix, rw_w_rkvz=rw_w_rkvz, rw_w0=rw_w0, rw_w1=rw_w1,
             rw_w2=rw_w2, rw_a0=rw_a0, rw_a1=rw_a1, rw_a2=rw_a2, rw_v0=rw_v0, rw_v1=rw_v1, rw_v2=rw_v2,
             rw_g1=rw_g1, rw_g2=rw_g2, rw_k_k=rw_k_k, rw_k_a=rw_k_a, rw_r_k=rw_r_k,
             rw_ln_w=rw_ln_w, rw_ln_b=rw_ln_b, rw_w_o=rw_w_o)
    for name in MATMUL_WEIGHTS:
        w[name] = w[name].astype(BF16)
    n_even, n_odd = state_hgrn.shape[0], state_wkv.shape[0]
    lbs = jax.nn.softmax(hgrn_lower_bounds, axis=0)
    w_in = w["even_w_in"]
    f_lo, f_hi = 2 * A_WIDTH + B_WIDTH, 2 * A_WIDTH + 2 * B_WIDTH
    prep = dict(lbs=jnp.cumsum(lbs, axis=0) - lbs[0], s5=[_s5_params(w, j) for j in range(n_even)],
                w_in=jnp.concatenate([w_in[..., :f_lo], w_in[..., f_hi:], w_in[..., f_lo:f_hi]], axis=-1))
    bp, bs, ts = x_prompt.shape[0], x_sample.shape[0], x_sample.shape[1]
    assert ts == SAMPLE_CFG["t_valid"]
    z_re = jnp.zeros((n_even, bp, A_GROUPS, A_STATE), F32)
    z_hg = jnp.zeros((n_even, bp, B_HEADS, B_DIM, B_DIM), F32)
    z_wkv = jnp.zeros((n_odd, bp, C_HEADS, C_HEAD, C_HEAD), F32)
    z_sh = jnp.zeros((n_odd, bp, D_MODEL), F32)
    prompt = _trunk(x_prompt, z_re, z_re, z_hg, z_wkv, z_sh, w, prep, PROMPT_CFG)
    xs = jnp.pad(x_sample, ((0, 0), (0, SAMPLE_PAD - ts), (0, 0)))
    sample = _trunk(xs, state_ssm_re, state_ssm_im, state_hgrn, state_wkv, state_shift, w, prep, SAMPLE_CFG)
    y_sample = sample[0][:, :ts]
    return (prompt[0], y_sample, *prompt[1:], *sample[1:])
```

```python
import functools
import math

import jax
import jax.numpy as jnp
from jax import lax
from jax.experimental import pallas as pl
from jax.experimental.pallas import tpu as pltpu

F32 = jnp.float32
BF16 = jnp.bfloat16

D_MODEL = 1024
A_WIDTH = 512
A_GROUP = 16
A_GROUPS = 32
A_STATE = 64
A_LANES = A_GROUPS * A_STATE
A_BLOCKS = 4
B_WIDTH = 512
PROJ_U, PROJ_ZA, PROJ_Q, PROJ_I, PROJ_ZB = range(5)
B_HEADS = 4
B_DIM = 128
C_HEADS = 16
C_HEAD = 64
C_PACK = 4
SEG_ROWS = 128
DECAY_SCALE = math.exp(-0.5)
RMS_EPS = 1e-6
GN_EPS = 64e-5
VMEM_LIMIT = 56 * 1024 * 1024
LANES = 128


def _dot(a, b):
    return jnp.dot(a.astype(b.dtype), b, preferred_element_type=F32)


def _bdot(a, b, contract):
    return lax.dot_general(a, b, (contract, ((0,), (0,))), preferred_element_type=F32)


def _nn(a, b):
    return _bdot(a, b, ((2,), (1,)))


def _nt(a, b):
    return _bdot(a, b, ((2,), (2,)))


def _tn(a, b):
    return _bdot(a, b, ((1,), (1,)))


def _rms(x, w):
    return x * lax.rsqrt(jnp.mean(x * x, axis=-1, keepdims=True) + RMS_EPS) * w


def _sigmoid(x):
    return 1.0 / (1.0 + jnp.exp(-x))


def _silu(x):
    return x * _sigmoid(x)


def _params(sem):
    return pltpu.CompilerParams(dimension_semantics=sem, vmem_limit_bytes=VMEM_LIMIT)


def _stacked_state_specs(prev, state_shape, n_layers, layer, bb):
    tail = (0,) * (len(state_shape) - 1)
    shape = jax.ShapeDtypeStruct((n_layers, *state_shape), F32)
    block = (bb, *state_shape[1:])
    if prev is None:
        return pl.BlockSpec((n_layers, *block), lambda i, j: (0, i, *tail)), shape, layer, False
    return pl.BlockSpec((1, *block), lambda i, j: (layer, i, *tail)), shape, 0, True


def _write_state_slab(ref, slot, aliased, value):
    if not aliased:
        for other in range(ref.shape[0]):
            if other != slot:
                ref[other] = jnp.zeros(ref.shape[1:], F32)
    ref[slot] = value


def _tri_masks(c):
    row = lax.broadcasted_iota(jnp.int32, (c, c), 0)
    col = lax.broadcasted_iota(jnp.int32, (c, c), 1)
    return row >= col, row > col, row == col


def _split_heads(x, width):
    bb, _, d = x.shape
    return jnp.stack([x[i, :, h * width:(h + 1) * width] for i in range(bb) for h in range(d // width)], axis=0)


def _merge_heads(x, bb):
    h = x.shape[0] // bb
    return jnp.stack([jnp.concatenate([x[i * h + j] for j in range(h)], axis=-1) for i in range(bb)], axis=0)


def _cumsum_time(tri_f, x):
    hi = x.astype(BF16).astype(F32)
    rest = x - hi
    mid = rest.astype(BF16).astype(F32)
    lo = rest - mid
    return jnp.stack([sum(jnp.dot(tri_f, p[i], preferred_element_type=F32) for p in (hi, mid, lo))
                      for i in range(x.shape[0])], axis=0)


def _norm_matmul_kernel(x_ref, nw_ref, w_ref, rest_ref, f_ref):
    res = _dot(_rms(x_ref[...], nw_ref[...]), w_ref[...])
    n_rest = rest_ref.shape[-1]
    rest_ref[...] = res[:, :n_rest].astype(rest_ref.dtype)
    f_ref[...] = res[:, n_rest:]


def _norm_matmul(x, nw, w, *, tm, n_f32, act_dtype):
    m, k = x.shape
    n = w.shape[1]
    row = lambda width: pl.BlockSpec((tm, width), lambda i: (i, 0))
    return pl.pallas_call(
        _norm_matmul_kernel, grid=(m // tm,),
        in_specs=[row(k), pl.BlockSpec((1, k), lambda i: (0, 0)), pl.BlockSpec((k, n), lambda i: (0, 0))],
        out_specs=[row(n - n_f32), row(n_f32)],
        out_shape=[jax.ShapeDtypeStruct((m, n - n_f32), act_dtype), jax.ShapeDtypeStruct((m, n_f32), F32)],
        compiler_params=_params(("parallel",)), name="norm_in_proj",
    )(x, nw, w)


def _proj_residual_kernel(*refs, n_x, final_norm):
    h_ref, x_refs, w_ref = refs[0], refs[1:1 + n_x], refs[1 + n_x]
    x = jnp.concatenate([r[...] for r in x_refs], axis=-1) if n_x > 1 else x_refs[0][...]
    h = h_ref[...] + _dot(x, w_ref[...])
    if final_norm:
        nw_ref, h_out, y_out = refs[2 + n_x:]
        y_out[...] = _rms(h, nw_ref[...])
    else:
        h_out = refs[2 + n_x]
    h_out[...] = h


def _proj_residual(h, xs, w, final_nw=None, *, tm):
    m, n = h.shape
    final_norm = final_nw is not None
    row = lambda width: pl.BlockSpec((tm, width), lambda i: (i, 0))
    in_specs = [row(n)] + [row(x.shape[1]) for x in xs] + [pl.BlockSpec(w.shape, lambda i: (0, 0))]
    args = [h, *xs, w]
    out_specs, out_shape = [row(n)], [jax.ShapeDtypeStruct((m, n), F32)]
    if final_norm:
        in_specs.append(pl.BlockSpec((1, n), lambda i: (0, 0)))
        args.append(final_nw)
        out_specs.append(row(n))
        out_shape.append(jax.ShapeDtypeStruct((m, n), F32))
    return pl.pallas_call(
        functools.partial(_proj_residual_kernel, n_x=len(xs), final_norm=final_norm), grid=(m // tm,),
        in_specs=in_specs, out_specs=out_specs, out_shape=out_shape,
        compiler_params=_params(("parallel",)), name="out_proj_residual",
    )(*args)


def _s5_kernel(u_ref, za_ref, h0r_ref, h0i_ref, ar_ref, ai_ref, bb_ref, c_ref,
               d_ref, gw_ref, gb_ref, out_ref, hfr_ref, hfi_ref, hre, him, bur, bui, st_r, st_i, wbm, wtm,
               *, bsz, tc, n_steps):
    i = pl.program_id(1)
    rows = bsz * tc
    n_tiles = A_LANES // LANES
    u_tiles = A_WIDTH // LANES
    per = n_tiles // A_BLOCKS
    assert A_WIDTH // A_BLOCKS == LANES

    @pl.when(i == 0)
    def _():
        for cb in range(n_tiles):
            st_r[cb] = h0r_ref[:, cb * LANES:(cb + 1) * LANES]
            st_i[cb] = h0i_ref[:, cb * LANES:(cb + 1) * LANES]

    def bm(t):
        return pl.ds(t, bsz, stride=tc)

    def tm(t):
        return pl.ds(pl.multiple_of(t * bsz, bsz), bsz)

    def reorder(src, dst, src_rows, dst_rows):
        def body(t, carry):
            for q in range(u_tiles):
                dst.at[q][dst_rows(t), :] = src.at[q][src_rows(t), :]
            return carry
        lax.fori_loop(0, tc, body, 0, unroll=min(tc, 8))

    u_bm = u_ref[...].astype(F32).reshape(rows, A_WIDTH)
    for q in range(u_tiles):
        wbm[q] = u_bm[:, q * LANES:(q + 1) * LANES]
    reorder(wbm, wtm, bm, tm)

    for j in range(A_BLOCKS):
        bu = _dot(wtm[j], bb_ref[j])
        for q in range(per):
            bur[j * per + q] = bu[:, q * LANES:(q + 1) * LANES]
            bui[j * per + q] = bu[:, (per + q) * LANES:(per + q + 1) * LANES]

    def advance(t, prev):
        new = []
        for cb in range(n_tiles):
            lanes = slice(cb * LANES, (cb + 1) * LANES)
            ar, ai = ar_ref[:, lanes], ai_ref[:, lanes]
            pr, pi = prev[2 * cb], prev[2 * cb + 1]
            hr = ar * pr - ai * pi + bur[cb, tm(t), :]
            hi = ar * pi + ai * pr + bui[cb, tm(t), :]
            hre[cb, tm(t), :] = hr
            him[cb, tm(t), :] = hi
            new += [hr, hi]
        return tuple(new)

    state = tuple(ref[cb] for cb in range(n_tiles) for ref in (st_r, st_i))
    if n_steps <= 8:
        for t in range(n_steps):
            state = advance(t, state)
    else:
        state = lax.fori_loop(0, n_steps, advance, state)
    for cb in range(n_tiles):
        st_r[cb] = state[2 * cb]
        st_i[cb] = state[2 * cb + 1]
    if n_steps < tc:
        pad = pl.ds(n_steps * bsz, (tc - n_steps) * bsz)
        for cb in range(n_tiles):
            hre[cb, pad, :] = bur[cb, pad, :]
            him[cb, pad, :] = bui[cb, pad, :]

    def block(j):
        return jnp.concatenate([ref[j * per + q] for ref in (hre, him) for q in range(per)], axis=-1)

    u_tm = jnp.concatenate([wtm[q] for q in range(u_tiles)], axis=-1)
    y = jnp.concatenate([_dot(block(j), c_ref[j]) for j in range(A_BLOCKS)], axis=-1) + d_ref[...] * u_tm
    y = 0.5 * y * (1.0 + jnp.tanh(math.sqrt(2.0 / math.pi) * (y + 0.044715 * (y * y * y))))
    y = y * _sigmoid(_dot(y, gw_ref[...]) + gb_ref[...])
    for q in range(u_tiles):
        wtm[q] = y[:, q * LANES:(q + 1) * LANES]
    reorder(wtm, wbm, tm, bm)
    y_bm = jnp.concatenate([wbm[q] for q in range(u_tiles)], axis=-1)
    out = y_bm * _silu(za_ref[...].astype(F32).reshape(rows, A_WIDTH))
    out_ref[...] = out.reshape(bsz, tc, A_WIDTH).astype(out_ref.dtype)

    @pl.when(i == pl.num_programs(1) - 1)
    def _():
        for cb in range(n_tiles):
            hfr_ref[:, cb * LANES:(cb + 1) * LANES] = st_r[cb]
            hfi_ref[:, cb * LANES:(cb + 1) * LANES] = st_i[cb]


def _s5_mixer(proj, h0r, h0i, sp, *, bb, tc, t_valid):
    bsz, t, _ = proj.shape
    n_steps = tc if t_valid is None else t_valid
    full = lambda a: pl.BlockSpec(a.shape, lambda b, i: (0,) * a.ndim)
    col = lambda c: pl.BlockSpec((bb, tc, A_WIDTH), lambda b, i: (b, i, c))
    st = pl.BlockSpec((bb, A_LANES), lambda b, i: (b, 0))
    assert (PROJ_U, PROJ_ZA) == (0, 1)
    weights = [sp["ab_re"], sp["ab_im"], sp["bb"], sp["c"], sp["d"], sp["glu_w"], sp["glu_b"]]
    state = jax.ShapeDtypeStruct((bsz, A_LANES), F32)
    return pl.pallas_call(
        functools.partial(_s5_kernel, bsz=bb, tc=tc, n_steps=n_steps), grid=(bsz // bb, t // tc),
        in_specs=[col(0), col(1), st, st] + [full(a) for a in weights],
        out_specs=[col(0), st, st],
        out_shape=[jax.ShapeDtypeStruct((bsz, t, A_WIDTH), proj.dtype), state, state],
        scratch_shapes=[pltpu.VMEM((A_LANES // LANES, bb * tc, LANES), F32)] * 4
        + [pltpu.VMEM((A_LANES // LANES, bb, LANES), F32)] * 2
        + [pltpu.VMEM((A_WIDTH // LANES, bb * tc, LANES), F32)] * 2,
        compiler_params=_params(("parallel", "arbitrary")), name="s5_mixer",
    )(proj, proj, h0r, h0i, *weights)


def _hgrn_kernel(*refs, bb, chunk, n_chunks, t_valid, use_lb, slot, aliased):
    q_ref, f_ref, i_ref, zb_ref, lb_ref, nw_ref, s0_ref = refs[:7]
    out_ref, sfin_ref, s_scr = refs[-3:]
    c = chunk
    tb = pl.program_id(1)

    @pl.when(tb == 0)
    def _():
        s_scr[...] = jnp.swapaxes(s0_ref[0].reshape(bb * B_HEADS, B_DIM, B_DIM), 1, 2)

    incl, _, _ = _tri_masks(c)
    tri_f = incl.astype(F32)

    def body(ci, carry):
        off = pl.multiple_of(ci * c, c)
        sl = pl.ds(off, c)
        q = _silu(q_ref[:, sl, :].astype(F32))
        f = f_ref[:, sl, :]
        e = jnp.exp(-jnp.abs(f))
        r = 1.0 / (1.0 + e)
        k = jnp.where(f >= 0.0, e * r, r)
        if use_lb:
            lb = lb_ref[...]
            log_f = jnp.log(lb + (1.0 - lb) * jnp.where(f >= 0.0, r, e * r))
            k = (1.0 - lb) * k
        else:
            log_f = jnp.minimum(f, 0.0) - jnp.log1p(e)
        if t_valid is not None:
            tok = tb * (n_chunks * c) + off + lax.broadcasted_iota(jnp.int32, (1, c, 1), 1)
            log_f = jnp.where(tok < t_valid, log_f, 0.0)
            k = jnp.where(tok < t_valid, k, 0.0)
        cum = _cumsum_time(tri_f, log_f)
        last = cum[:, c - 1:c, :]
        mid = cum[:, c // 2 - 1:c // 2, :]
        rel = jnp.exp(cum - mid)
        q_mid = q * rel
        k_mid = k * (1.0 / rel)
        qm = _split_heads(q_mid, B_DIM)
        kh = _split_heads(k_mid, B_DIM)
        qh = _split_heads(q_mid * jnp.exp(mid), B_DIM)
        ke = _split_heads(k_mid * jnp.exp(last - mid), B_DIM)
        vh = _split_heads(i_ref[:, sl, :].astype(F32), B_DIM)
        gl = _split_heads(jnp.exp(last), B_DIM)
        s = s_scr[...]
        att = jnp.where(incl[None], _nt(qm, kh), 0.0)
        o = _nn(att, vh) + _nt(qh, s)
        s_scr[...] = s * gl + _tn(vh, ke)
        o = o * lax.rsqrt(jnp.mean(o * o, axis=-1, keepdims=True) + RMS_EPS) * nw_ref[...]
        out_ref[:, sl, :] = (_merge_heads(o, bb) * _silu(zb_ref[:, sl, :].astype(F32))).astype(out_ref.dtype)
        return carry

    lax.fori_loop(0, n_chunks, body, 0)

    @pl.when(tb == pl.num_programs(1) - 1)
    def _():
        _write_state_slab(sfin_ref, slot, aliased,
                          jnp.swapaxes(s_scr[...], 1, 2).reshape(bb, B_HEADS, B_DIM, B_DIM))


def _hgrn_mixer(proj, proj_f, s0, lb, nw, prev, *, layer, n_layers, bb, tblock, chunk, t_valid, use_lb):
    bsz, t, _ = proj.shape
    st_out, st_shape, slot, aliased = _stacked_state_specs(prev, s0.shape[1:], n_layers, layer, bb)
    extra = ([pl.BlockSpec(memory_space=pl.ANY)], [prev], {7: 1}) if aliased else ([], [], {})
    col = lambda c: pl.BlockSpec((bb, tblock, B_WIDTH), lambda i, j: (i, j, c))
    vec = lambda a: pl.BlockSpec(a.shape, lambda i, j: (0, 0))
    st = pl.BlockSpec((1, bb, B_HEADS, B_DIM, B_DIM), lambda i, j: (layer, i, 0, 0, 0))
    return pl.pallas_call(
        functools.partial(_hgrn_kernel, bb=bb, chunk=chunk, n_chunks=tblock // chunk, t_valid=t_valid,
                          use_lb=use_lb, slot=slot, aliased=aliased),
        grid=(bsz // bb, t // tblock),
        in_specs=[col(PROJ_Q), col(0), col(PROJ_I), col(PROJ_ZB), vec(lb), vec(nw), st] + extra[0],
        out_specs=[pl.BlockSpec((bb, tblock, B_WIDTH), lambda i, j: (i, j, 0)), st_out],
        out_shape=[jax.ShapeDtypeStruct((bsz, t, B_WIDTH), proj.dtype), st_shape],
        scratch_shapes=[pltpu.VMEM((bb * B_HEADS, B_DIM, B_DIM), F32)],
        input_output_aliases=extra[2],
        compiler_params=_params(("parallel", "arbitrary")), name="hgrn2_mixer",
    )(proj, proj_f, proj, proj, lb, nw, s0, *extra[1])


def _rwkv_in_kernel(*refs, tblock, last_row, vres):
    h_ref, nw_ref, sh_ref, mix_ref, wp_ref, w0, w1, w2, a0, a1, a2, g1, g2 = refs[:13]
    if vres:
        v0, v1, v2, vf_ref = refs[13:17]
    r_ref, k_ref, v_ref, gate_ref, lw_ref, al_ref, last_ref, carry = refs[-8:]

    @pl.when(pl.program_id(1) == 0)
    def _():
        carry[...] = sh_ref[...]

    xn = _rms(h_ref[...], nw_ref[...])
    row = lax.broadcasted_iota(jnp.int32, (1, tblock, 1), 1)
    xx = jnp.where(row == 0, carry[...], pltpu.roll(xn, 1, axis=1)) - xn
    carry[...] = xn[:, tblock - 1:tblock, :]
    last_ref[...] = xn[:, last_row:last_row + 1, :]
    shape = xn.shape
    xn = xn.reshape(-1, shape[-1])
    xx = xx.reshape(-1, shape[-1])
    mixed = lambda m: xn + xx * mix_ref[m:m + 1, :]
    store = lambda ref, val: ref.__setitem__(Ellipsis, val.reshape(shape).astype(ref.dtype))
    x_v, x_g = mixed(3), mixed(5)
    store(r_ref, _dot(mixed(0), wp_ref[0]))
    store(k_ref, _dot(mixed(2), wp_ref[1]))
    v = _dot(x_v, wp_ref[2])
    if vres:
        v_first = vf_ref[...].reshape(v.shape).astype(F32)
        v = v + (v_first - v) * _sigmoid(v0[...] + _dot(_dot(x_v, v1[...]), v2[...]))
    store(v_ref, v)
    store(gate_ref, _dot(_sigmoid(_dot(x_g, g1[...])), g2[...]) * _silu(_dot(x_g, wp_ref[3])))
    store(lw_ref, -DECAY_SCALE * _sigmoid(w0[...] + _dot(jnp.tanh(_dot(mixed(1), w1[...])), w2[...])))
    store(al_ref, _sigmoid(a0[...] + _dot(_dot(mixed(4), a1[...]), a2[...])))


def _rwkv_in(h, nw, shift, mix, wp, lora_w, v_first, *, bb, tblock, t_valid, act_dtype):
    bsz, t, d = h.shape
    vres = v_first is not None
    tok = pl.BlockSpec((bb, tblock, d), lambda i, j: (i, j, 0))
    one = pl.BlockSpec((bb, 1, d), lambda i, j: (i, 0, 0))
    full = lambda a: pl.BlockSpec(a.shape, lambda i, j: (0,) * a.ndim)
    in_specs = [tok, full(nw), one, full(mix), full(wp)] + [full(a) for a in lora_w]
    args = [h, nw, shift.reshape(bsz, 1, d), mix, wp, *lora_w]
    if vres:
        in_specs.append(tok)
        args.append(v_first)
    act = lambda dt: jax.ShapeDtypeStruct((bsz, t, d), dt)
    last_row = (tblock if t_valid is None else t_valid) - 1
    return pl.pallas_call(
        functools.partial(_rwkv_in_kernel, tblock=tblock, last_row=last_row, vres=vres),
        grid=(bsz // bb, t // tblock),
        in_specs=in_specs, out_specs=[tok] * 6 + [one],
        out_shape=[act(act_dtype)] * 4 + [act(F32)] * 2 + [jax.ShapeDtypeStruct((bsz, 1, d), F32)],
        scratch_shapes=[pltpu.VMEM((bb, 1, d), F32)],
        compiler_params=_params(("parallel", "arbitrary")), name="rwkv_in_proj",
    )(*args)


def _rwkv_kernel(*refs, bb, chunk, n_chunks, t_valid, slot, aliased):
    r_ref, k_ref, v_ref, gate_ref, lw_ref, al_ref, kk_ref, ka_ref, rk_ref, lnw_ref, lnb_ref, seg_ref, s0_ref = refs[:13]
    y_ref, sfin_ref, s_scr = refs[-3:]
    c = chunk
    gw = C_PACK * C_HEAD
    aw = C_PACK * c
    n_groups = D_MODEL // gw
    n_sq = int(math.log2(c)) - 1
    tb = pl.program_id(1)

    @pl.when(tb == 0)
    def _():
        for b in range(bb):
            s_scr[b] = jnp.concatenate([s0_ref[b, h] for h in range(C_HEADS)], axis=-1)

    row = lax.broadcasted_iota(jnp.int32, (c, aw), 0)
    col = lax.broadcasted_iota(jnp.int32, (c, aw), 1) % c
    incl, strict = row >= col, row > col
    eye = (row == col).astype(F32)
    tri_f = _tri_masks(c)[0].astype(F32)

    def bd_mask(rows_per_head, lanes, lanes_per_head):
        r = lax.broadcasted_iota(jnp.int32, (C_PACK * rows_per_head, lanes), 0) // rows_per_head
        l = lax.broadcasted_iota(jnp.int32, (C_PACK * rows_per_head, lanes), 1) // lanes_per_head
        return r == l

    bd_c, bd_v, bd_p = bd_mask(c, gw, C_HEAD), bd_mask(C_HEAD, gw, C_HEAD), bd_mask(c, aw, c)
    lane_head = lax.broadcasted_iota(jnp.int32, (1, gw), 1) // C_HEAD
    seg = seg_ref[...]

    def bd(y, mask):
        return jnp.where(mask, jnp.tile(y, (C_PACK, 1)), 0.0)

    def seg_sum(xs, pieces=1):
        rows = xs[0].shape[0]
        per_dot = max(1, SEG_ROWS // (rows * pieces))
        result = []
        for start in range(0, len(xs), per_dot):
            parts = []
            for x in xs[start:start + per_dot]:
                for _ in range(pieces):
                    parts.append(x.astype(BF16).astype(F32))
                    x = x - parts[-1]
            out = jnp.dot(jnp.concatenate(parts, axis=0), seg, preferred_element_type=F32)
            blocks = [out[i * rows:(i + 1) * rows] for i in range(len(parts))]
            result += [sum(blocks[i:i + pieces]) for i in range(0, len(parts), pieces)]
        return result

    def nn(a, b):
        return jnp.dot(a, b, preferred_element_type=F32)

    def nt(a, b):
        return lax.dot_general(a, b, (((1,), (1,)), ((), ())), preferred_element_type=F32)

    def tn(a, b):
        return lax.dot_general(a, b, (((0,), (0,)), ((), ())), preferred_element_type=F32)

    def cat(a, b):
        return jnp.concatenate([a, b], axis=0)

    def body(ci, carry):
        off = pl.multiple_of(ci * c, c)
        sl = pl.ds(off, c)
        inst = [(b, slice(g * gw, (g + 1) * gw)) for b in range(bb) for g in range(n_groups)]
        r_l, k_l, v_l, kk_l, al_l, rk_l, gi_l, gv_l, gm_l = ([] for _ in range(9))
        for b in range(bb):
            r = r_ref[b, sl, :].astype(F32)
            k = k_ref[b, sl, :].astype(F32)
            v = v_ref[b, sl, :].astype(F32)
            lw = lw_ref[b, sl, :]
            al = al_ref[b, sl, :]
            kk_raw = k * kk_ref[...]
            k = k * (1.0 + (al - 1.0) * ka_ref[...])
            rk = r * k * rk_ref[...]
            if t_valid is not None:
                ok = tb * (n_chunks * c) + off + lax.broadcasted_iota(jnp.int32, (c, 1), 0) < t_valid
                lw = jnp.where(ok, lw, 0.0)
                k = jnp.where(ok, k, 0.0)
                al = jnp.where(ok, al, 0.0)
            cum = _cumsum_time(tri_f, lw[None])[0]
            g_in, g_inv, g_m = jnp.exp(cum), jnp.exp(-cum), jnp.exp(cum - lw)
            for g in range(n_groups):
                ln = slice(g * gw, (g + 1) * gw)
                for lst, val in ((r_l, r), (k_l, k), (v_l, v), (kk_l, kk_raw), (al_l, al), (rk_l, rk),
                                 (gi_l, g_in), (gv_l, g_inv), (gm_l, g_m)):
                    lst.append(val[:, ln])
        n = len(inst)
        each = range(n)
        ss_l = seg_sum([kk_l[i] * kk_l[i] for i in each], pieces=2)
        kk_l = [kk_l[i] / jnp.maximum(jnp.sqrt(ss_l[i]), 1e-12) for i in each]
        kt_l = [k_l[i] * gv_l[i] for i in each]
        bt_l = [kk_l[i] * al_l[i] * gv_l[i] for i in each]
        x_l = [cat(-kk_l[i] * gm_l[i], r_l[i] * gi_l[i]) for i in each]
        s_l = [s_scr[b, :, ln] for b, ln in inst]
        a_all = [nt(x_l[i], jnp.concatenate([bd(bt_l[i], bd_c), bd(kt_l[i], bd_c), bd(s_l[i], bd_v)], axis=0))
                 for i in each]
        a_b = [a[:, :aw] for a in a_all]
        a_k = [a[:, aw:2 * aw] for a in a_all]
        x_s = [a[:, 2 * aw:] for a in a_all]
        n_ab = [jnp.where(strict, a_b[i][:c], 0.0) for i in each]
        a_rb = [jnp.where(incl, a_b[i][c:], 0.0) for i in each]
        a_kk = [cat(jnp.where(strict, a_k[i][:c], 0.0), jnp.where(incl, a_k[i][c:], 0.0)) for i in each]
        a_v = [nn(a_kk[i], bd(v_l[i], bd_c)) for i in each]
        w_m = [x_s[i][:c] + a_v[i][:c] for i in each]
        t_m = [eye + n_ab[i] for i in each]
        p = n_ab
        if n_sq >= 1:
            p = [nn(p[i], bd(p[i], bd_p)) for i in each]
            for _ in range(n_sq - 1):
                both = [nn(cat(p[i], t_m[i]), bd(p[i], bd_p)) for i in each]
                p = [both[i][:c] for i in each]
                t_m = [t_m[i] + both[i][c:] for i in each]
            t_m = [t_m[i] + nn(t_m[i], bd(p[i], bd_p)) for i in each]
        u = [nn(t_m[i], bd(w_m[i], bd_c)) for i in each]
        y = [x_s[i][c:] + a_v[i][c:] for i in each]
        y_u = [nn(a_rb[i], bd(u[i], bd_c)) for i in each]
        full = [tn(cat(v_l[i], u[i]), cat(kt_l[i], bt_l[i])) for i in each]
        for i, (b, ln) in enumerate(inst):
            upd = sum(jnp.where(lane_head == h, full[i][h * C_HEAD:(h + 1) * C_HEAD], 0.0) for h in range(C_PACK))
            s_scr[b, :, ln] = (s_l[i] + upd) * gi_l[i][c - 1:c]
        y = [y[i] + y_u[i] for i in each]
        sums = seg_sum([cat(y[i], rk_l[i]) for i in each])
        dev = [y[i] - sums[i][:c] * (1.0 / C_HEAD) for i in each]
        var = seg_sum([dev[i] * dev[i] for i in each])
        bonus = [sums[i][c:] * v_l[i] for i in each]
        yn = [dev[i] * lax.rsqrt(var[i] * (1.0 / C_HEAD) + GN_EPS) for i in each]
        for b in range(bb):
            mine = range(b * n_groups, (b + 1) * n_groups)
            yn_b = jnp.concatenate([yn[i] for i in mine], axis=-1)
            bonus_b = jnp.concatenate([bonus[i] for i in mine], axis=-1)
            y_ref[b, sl, :] = ((yn_b * lnw_ref[...] + lnb_ref[...] + bonus_b)
                               * gate_ref[b, sl, :].astype(F32)).astype(y_ref.dtype)
        return carry

    lax.fori_loop(0, n_chunks, body, 0)

    @pl.when(tb == pl.num_programs(1) - 1)
    def _():
        final = jnp.stack([jnp.stack([s_scr[b, :, h * C_HEAD:(h + 1) * C_HEAD] for h in range(C_HEADS)])
                           for b in range(bb)])
        _write_state_slab(sfin_ref, slot, aliased, final)


def _rwkv_mixer(r, k, v, gate, lw, al, vecs, s0, prev, *, layer, n_layers, bb, tblock, chunk, t_valid):
    bsz, t, d = r.shape
    st_out, st_shape, slot, aliased = _stacked_state_specs(prev, s0.shape, n_layers, layer, bb)
    tok = pl.BlockSpec((bb, tblock, d), lambda i, j: (i, j, 0))
    vec = pl.BlockSpec((1, d), lambda i, j: (0, 0))
    st = pl.BlockSpec((bb, C_HEADS, C_HEAD, C_HEAD), lambda i, j: (i, 0, 0, 0))
    seg = jnp.kron(jnp.eye(C_PACK, dtype=F32), jnp.ones((C_HEAD, C_HEAD), F32))
    in_specs = [tok] * 6 + [vec] * 5 + [pl.BlockSpec(seg.shape, lambda i, j: (0, 0)), st]
    args = [r, k, v, gate, lw, al, *vecs, seg, s0]
    aliases = {}
    if aliased:
        aliases = {len(args): 1}
        in_specs.append(pl.BlockSpec(memory_space=pl.ANY))
        args.append(prev)
    return pl.pallas_call(
        functools.partial(_rwkv_kernel, bb=bb, chunk=chunk, n_chunks=tblock // chunk, t_valid=t_valid,
                          slot=slot, aliased=aliased),
        grid=(bsz // bb, t // tblock),
        in_specs=in_specs, out_specs=[tok, st_out], out_shape=[jax.ShapeDtypeStruct((bsz, t, d), r.dtype), st_shape],
        input_output_aliases=aliases,
        scratch_shapes=[pltpu.VMEM((bb, C_HEAD, d), F32)],
        compiler_params=_params(("parallel", "arbitrary")), name="rwkv7_mixer",
    )(*args)


def _s5_params(w, j):
    lr = jnp.minimum(w["ssm_lambda_re"][j], -1e-4)
    li = w["ssm_lambda_im"][j]
    step = jnp.exp(w["ssm_log_step"][j])[:, None]
    mag = jnp.exp(lr * step)
    ab_re = mag * jnp.cos(li * step)
    ab_im = mag * jnp.sin(li * step)
    den = lr * lr + li * li
    nr = ab_re - 1.0
    cr = (nr * lr + ab_im * li) / den
    ci = (ab_im * lr - nr * li) / den
    b_re, b_im = w["ssm_b_re"][j], w["ssm_b_im"][j]
    bb_re = cr[..., None] * b_re - ci[..., None] * b_im
    bb_im = cr[..., None] * b_im + ci[..., None] * b_re
    gpb = A_GROUPS // A_BLOCKS
    eye = jnp.eye(gpb, dtype=F32)

    def pack_in(bb):
        bb = bb.reshape(A_BLOCKS, gpb, A_STATE, A_GROUP)
        return jnp.einsum("bgph,gk->bghkp", bb, eye).reshape(A_BLOCKS, gpb * A_GROUP, gpb * A_STATE)

    def pack_out(c):
        c = c.reshape(A_BLOCKS, gpb, A_GROUP, A_STATE)
        return jnp.einsum("bghp,gk->bgpkh", c, eye).reshape(A_BLOCKS, gpb * A_STATE, gpb * A_GROUP)

    return dict(ab_re=ab_re.reshape(1, A_LANES), ab_im=ab_im.reshape(1, A_LANES),
                bb=jnp.concatenate([pack_in(bb_re), pack_in(bb_im)], axis=-1).astype(BF16),
                c=jnp.concatenate([pack_out(w["ssm_c_re"][j]), -pack_out(w["ssm_c_im"][j])], axis=1).astype(BF16),
                d=w["ssm_d"][j].reshape(1, A_WIDTH), glu_w=w["ssm_glu_w"][j],
                glu_b=w["ssm_glu_b"][j].reshape(1, A_WIDTH))


def _trunk(x, ssm_re, ssm_im, hgrn, wkv, shift, w, prep, cfg):
    bsz, t, d = x.shape
    t_valid = cfg["t_valid"]
    m = bsz * t
    h = x.reshape(m, d)
    n_re, n_im, n_sh = [], [], []
    hg_all = wkv_all = None
    lbs = prep["lbs"]
    v_first = None
    y = None
    depth = w["norm_w"].shape[0]
    for layer in range(depth):
        j = layer // 2
        nw = w["norm_w"][layer].reshape(1, d)
        last = layer == depth - 1
        final_nw = w["final_norm_w"].reshape(1, d) if last else None
        if layer % 2 == 0:
            proj, proj_f = _norm_matmul(h, nw, prep["w_in"][j], tm=cfg["tm"] // 2, n_f32=B_WIDTH,
                                        act_dtype=cfg["act_dtype"])
            proj, proj_f = proj.reshape(bsz, t, -1), proj_f.reshape(bsz, t, -1)
            out_a, hr, hi = _s5_mixer(proj, ssm_re[j].reshape(bsz, A_LANES), ssm_im[j].reshape(bsz, A_LANES),
                                      prep["s5"][j], bb=cfg["s5_bb"], tc=cfg["s5_tc"], t_valid=t_valid)
            out_b, hg_all = _hgrn_mixer(proj, proj_f, hgrn, lbs[j].reshape(1, B_WIDTH),
                                        w["hgrn_norm_w"][j].reshape(1, B_DIM), hg_all, layer=j,
                                        n_layers=hgrn.shape[0], bb=cfg["hgrn_bb"], tblock=cfg["hgrn_tblock"],
                                        chunk=cfg["chunk"], t_valid=t_valid, use_lb=j > 0)
            res = _proj_residual(h, [out_a.reshape(m, A_WIDTH), out_b.reshape(m, B_WIDTH)], w["even_w_out"][j],
                                 final_nw, tm=cfg["tm"])
            n_re.append(hr.reshape(bsz, A_GROUPS, A_STATE))
            n_im.append(hi.reshape(bsz, A_GROUPS, A_STATE))
        else:
            row = lambda a: a.reshape(1, -1)
            lora_w = [row(w["rw_w0"][j]), w["rw_w1"][j], w["rw_w2"][j], row(w["rw_a0"][j]), w["rw_a1"][j],
                      w["rw_a2"][j], w["rw_g1"][j], w["rw_g2"][j]]
            if v_first is not None:
                lora_w += [row(w["rw_v0"][j - 1]), w["rw_v1"][j - 1], w["rw_v2"][j - 1]]
            r, k, v, gate, lw, al, x_last = _rwkv_in(h.reshape(bsz, t, d), nw, shift[j], w["rw_mix"][j],
                                                     w["rw_w_rkvz"][j], lora_w, v_first, bb=cfg["in_bb"],
                                                     tblock=cfg["in_tblock"], t_valid=t_valid,
                                                     act_dtype=cfg["act_dtype"])
            n_sh.append(x_last.reshape(bsz, d))
            if v_first is None:
                v_first = v
            vecs = [row(w[n][j]) for n in ("rw_k_k", "rw_k_a", "rw_r_k", "rw_ln_w", "rw_ln_b")]
            y_mix, wkv_all = _rwkv_mixer(r, k, v, gate, lw, al, vecs, wkv[j], wkv_all, layer=j,
                                         n_layers=wkv.shape[0], bb=cfg["bb"], tblock=cfg["tblock"],
                                         chunk=cfg["rwkv_chunk"], t_valid=t_valid)
            res = _proj_residual(h, [y_mix.reshape(m, d)], w["rw_w_o"][j], final_nw, tm=cfg["tm"])
        if last:
            h, y = res
        else:
            h = res[0]
    return (y.reshape(bsz, t, d), jnp.stack(n_re), jnp.stack(n_im), hg_all, wkv_all,
            jnp.stack(n_sh))


MATMUL_WEIGHTS = ("even_w_in", "even_w_out", "ssm_glu_w", "rw_w_rkvz", "rw_w1", "rw_w2", "rw_a1", "rw_a2",
                  "rw_v1", "rw_v2", "rw_g1", "rw_g2", "rw_w_o")
PROMPT_CFG = dict(t_valid=None, tm=1024, s5_bb=8, s5_tc=128, bb=2, tblock=256, chunk=32, rwkv_chunk=64,
                  hgrn_bb=8, hgrn_tblock=256, in_bb=1, in_tblock=512, act_dtype=BF16)
SAMPLE_PAD = 8
SAMPLE_CFG = dict(t_valid=4, tm=1024, s5_bb=32, s5_tc=SAMPLE_PAD, bb=8, tblock=SAMPLE_PAD, chunk=SAMPLE_PAD,
                  rwkv_chunk=SAMPLE_PAD, hgrn_bb=8, hgrn_tblock=SAMPLE_PAD, in_bb=32, in_tblock=SAMPLE_PAD,
                  act_dtype=F32)


def kernel(x_prompt, x_sample, state_ssm_re, state_ssm_im, state_hgrn, state_wkv, state_shift, norm_w, final_norm_w, even_w_in, even_w_out, ssm_lambda_re, ssm_lambda_im, ssm_log_step, ssm_b_re, ssm_b_im, ssm_c_re, ssm_c_im, ssm_d, ssm_glu_w, ssm_glu_b, hgrn_lower_bounds, hgrn_norm_w, rw_mix, rw_w_rkvz, rw_w0, rw_w1, rw_w2, rw_a0, rw_a1, rw_a2, rw_v0, rw_v1, rw_v2, rw_g1, rw_g2, rw_k_k, rw_k_a, rw_r_k, rw_ln_w, rw_ln_b, rw_w_o):
    w = dict(norm_w=norm_w, final_norm_w=final_norm_w, even_w_in=even_w_in, even_w_out=even_w_out,
             ssm_lambda_re=ssm_lambda_re, ssm_lambda_im=ssm_lambda_im, ssm_log_step=ssm_log_step,
             ssm_b_re=ssm_b_re, ssm_b_im=ssm_b_im, ssm_c_re=ssm_c_re, ssm_c_im=ssm_c_im, ssm_d=ssm_d,
             ssm_glu_w=ssm_glu_w, ssm_glu_b=ssm_glu_b, hgrn_lower_bounds=hgrn_lower_bounds,
             hgrn_norm_w=hgrn_norm_w, rw_mix=rw_mix, rw_w_rkvz=rw_w_rkvz, rw_w0=rw_w0, rw_w1=rw_w1,
             rw_w2=rw_w2, rw_a0=rw_a0, rw_a1=rw_a1, rw_a2=rw_a2, rw_v0=rw_v0, rw_v1=rw_v1, rw_v2=rw_v2,
             rw_g1=rw_g1, rw_g2=rw_g2, rw_k_k=rw_k_k, rw_k_a=rw_k_a, rw_r_k=rw_r_k,
             rw_ln_w=rw_ln_w, rw_ln_b=rw_ln_b, rw_w_o=rw_w_o)
    for name in MATMUL_WEIGHTS:
        w[name] = w[name].astype(BF16)
    n_even, n_odd = state_hgrn.shape[0], state_wkv.shape[0]
    lbs = jax.nn.softmax(hgrn_lower_bounds, axis=0)
    w_in = w["even_w_in"]
    f_lo, f_hi = 2 * A_WIDTH + B_WIDTH, 2 * A_WIDTH + 2 * B_WIDTH
    prep = dict(lbs=jnp.cumsum(lbs, axis=0) - lbs[0], s5=[_s5_params(w, j) for j in range(n_even)],
                w_in=jnp.concatenate([w_in[..., :f_lo], w_in[..., f_hi:], w_in[..., f_lo:f_hi]], axis=-1))
    bp, bs, ts = x_prompt.shape[0], x_sample.shape[0], x_sample.shape[1]
    assert ts == SAMPLE_CFG["t_valid"]
    z_re = jnp.zeros((n_even, bp, A_GROUPS, A_STATE), F32)
    z_hg = jnp.zeros((n_even, bp, B_HEADS, B_DIM, B_DIM), F32)
    z_wkv = jnp.zeros((n_odd, bp, C_HEADS, C_HEAD, C_HEAD), F32)
    z_sh = jnp.zeros((n_odd, bp, D_MODEL), F32)
    prompt = _trunk(x_prompt, z_re, z_re, z_hg, z_wkv, z_sh, w, prep, PROMPT_CFG)
    xs = jnp.pad(x_sample, ((0, 0), (0, SAMPLE_PAD - ts), (0, 0)))
    sample = _trunk(xs, state_ssm_re, state_ssm_im, state_hgrn, state_wkv, state_shift, w, prep, SAMPLE_CFG)
    y_sample = sample[0][:, :ts]
    return (prompt[0], y_sample, *prompt[1:], *sample[1:])
```

```python
import functools
import math

import jax
import jax.numpy as jnp
from jax import lax
from jax.experimental import pallas as pl
from jax.experimental.pallas import tpu as pltpu

F32 = jnp.float32
BF16 = jnp.bfloat16

D_MODEL = 1024
A_WIDTH = 512
A_GROUP = 16
A_GROUPS = 32
A_STATE = 64
A_LANES = A_GROUPS * A_STATE
A_BLOCKS = 4
B_WIDTH = 512
PROJ_U, PROJ_ZA, PROJ_Q, PROJ_I, PROJ_ZB = range(5)
B_HEADS = 4
B_DIM = 128
C_HEADS = 16
C_HEAD = 64
C_PACK = 4
SEG_ROWS = 128
DECAY_SCALE = math.exp(-0.5)
RMS_EPS = 1e-6
GN_EPS = 64e-5
VMEM_LIMIT = 56 * 1024 * 1024
LANES = 128


def _dot(a, b):
    return jnp.dot(a.astype(b.dtype), b, preferred_element_type=F32)


def _bdot(a, b, contract):
    return lax.dot_general(a, b, (contract, ((0,), (0,))), preferred_element_type=F32)


def _nn(a, b):
    return _bdot(a, b, ((2,), (1,)))


def _nt(a, b):
    return _bdot(a, b, ((2,), (2,)))


def _tn(a, b):
    return _bdot(a, b, ((1,), (1,)))


def _rms(x, w):
    return x * lax.rsqrt(jnp.mean(x * x, axis=-1, keepdims=True) + RMS_EPS) * w


def _sigmoid(x):
    return 1.0 / (1.0 + jnp.exp(-x))


def _silu(x):
    return x * _sigmoid(x)


def _params(sem):
    return pltpu.CompilerParams(dimension_semantics=sem, vmem_limit_bytes=VMEM_LIMIT)


def _stacked_state_specs(prev, state_shape, n_layers, layer, bb):
    tail = (0,) * (len(state_shape) - 1)
    shape = jax.ShapeDtypeStruct((n_layers, *state_shape), F32)
    block = (bb, *state_shape[1:])
    if prev is None:
        return pl.BlockSpec((n_layers, *block), lambda i, j: (0, i, *tail)), shape, layer, False
    return pl.BlockSpec((1, *block), lambda i, j: (layer, i, *tail)), shape, 0, True


def _write_state_slab(ref, slot, aliased, value):
    if not aliased:
        for other in range(ref.shape[0]):
            if other != slot:
                ref[other] = jnp.zeros(ref.shape[1:], F32)
    ref[slot] = value


def _tri_masks(c):
    row = lax.broadcasted_iota(jnp.int32, (c, c), 0)
    col = lax.broadcasted_iota(jnp.int32, (c, c), 1)
    return row >= col, row > col, row == col


def _split_heads(x, width):
    bb, _, d = x.shape
    return jnp.stack([x[i, :, h * width:(h + 1) * width] for i in range(bb) for h in range(d // width)], axis=0)


def _merge_heads(x, bb):
    h = x.shape[0] // bb
    return jnp.stack([jnp.concatenate([x[i * h + j] for j in range(h)], axis=-1) for i in range(bb)], axis=0)


def _cumsum_time(tri_f, x):
    hi = x.astype(BF16).astype(F32)
    rest = x - hi
    mid = rest.astype(BF16).astype(F32)
    lo = rest - mid
    return jnp.stack([sum(jnp.dot(tri_f, p[i], preferred_element_type=F32) for p in (hi, mid, lo))
                      for i in range(x.shape[0])], axis=0)


def _norm_matmul_kernel(x_ref, nw_ref, w_ref, rest_ref, f_ref):
    res = _dot(_rms(x_ref[...], nw_ref[...]), w_ref[...])
    n_rest = rest_ref.shape[-1]
    rest_ref[...] = res[:, :n_rest].astype(rest_ref.dtype)
    f_ref[...] = res[:, n_rest:]


def _norm_matmul(x, nw, w, *, tm, n_f32, act_dtype):
    m, k = x.shape
    n = w.shape[1]
    row = lambda width: pl.BlockSpec((tm, width), lambda i: (i, 0))
    return pl.pallas_call(
        _norm_matmul_kernel, grid=(m // tm,),
        in_specs=[row(k), pl.BlockSpec((1, k), lambda i: (0, 0)), pl.BlockSpec((k, n), lambda i: (0, 0))],
        out_specs=[row(n - n_f32), row(n_f32)],
        out_shape=[jax.ShapeDtypeStruct((m, n - n_f32), act_dtype), jax.ShapeDtypeStruct((m, n_f32), F32)],
        compiler_params=_params(("parallel",)), name="norm_in_proj",
    )(x, nw, w)


def _proj_residual_kernel(*refs, n_x, final_norm):
    h_ref, x_refs, w_ref = refs[0], refs[1:1 + n_x], refs[1 + n_x]
    x = jnp.concatenate([r[...] for r in x_refs], axis=-1) if n_x > 1 else x_refs[0][...]
    h = h_ref[...] + _dot(x, w_ref[...])
    if final_norm:
        nw_ref, h_out, y_out = refs[2 + n_x:]
        y_out[...] = _rms(h, nw_ref[...])
    else:
        h_out = refs[2 + n_x]
    h_out[...] = h


def _proj_residual(h, xs, w, final_nw=None, *, tm):
    m, n = h.shape
    final_norm = final_nw is not None
    row = lambda width: pl.BlockSpec((tm, width), lambda i: (i, 0))
    in_specs = [row(n)] + [row(x.shape[1]) for x in xs] + [pl.BlockSpec(w.shape, lambda i: (0, 0))]
    args = [h, *xs, w]
    out_specs, out_shape = [row(n)], [jax.ShapeDtypeStruct((m, n), F32)]
    if final_norm:
        in_specs.append(pl.BlockSpec((1, n), lambda i: (0, 0)))
        args.append(final_nw)
        out_specs.append(row(n))
        out_shape.append(jax.ShapeDtypeStruct((m, n), F32))
    return pl.pallas_call(
        functools.partial(_proj_residual_kernel, n_x=len(xs), final_norm=final_norm), grid=(m // tm,),
        in_specs=in_specs, out_specs=out_specs, out_shape=out_shape,
        compiler_params=_params(("parallel",)), name="out_proj_residual",
    )(*args)


def _s5_kernel(u_ref, za_ref, h0r_ref, h0i_ref, ar_ref, ai_ref, bb_ref, c_ref,
               d_ref, gw_ref, gb_ref, out_ref, hfr_ref, hfi_ref, hre, him, bur, bui, st_r, st_i, wbm, wtm,
               *, bsz, tc, n_steps):
    i = pl.program_id(1)
    rows = bsz * tc
    n_tiles = A_LANES // LANES
    u_tiles = A_WIDTH // LANES
    per = n_tiles // A_BLOCKS
    assert A_WIDTH // A_BLOCKS == LANES

    @pl.when(i == 0)
    def _():
        for cb in range(n_tiles):
            st_r[cb] = h0r_ref[:, cb * LANES:(cb + 1) * LANES]
            st_i[cb] = h0i_ref[:, cb * LANES:(cb + 1) * LANES]

    def bm(t):
        return pl.ds(t, bsz, stride=tc)

    def tm(t):
        return pl.ds(pl.multiple_of(t * bsz, bsz), bsz)

    def reorder(src, dst, src_rows, dst_rows):
        def body(t, carry):
            for q in range(u_tiles):
                dst.at[q][dst_rows(t), :] = src.at[q][src_rows(t), :]
            return carry
        lax.fori_loop(0, tc, body, 0, unroll=min(tc, 8))

    u_bm = u_ref[...].astype(F32).reshape(rows, A_WIDTH)
    for q in range(u_tiles):
        wbm[q] = u_bm[:, q * LANES:(q + 1) * LANES]
    reorder(wbm, wtm, bm, tm)

    for j in range(A_BLOCKS):
        bu = _dot(wtm[j], bb_ref[j])
        for q in range(per):
            bur[j * per + q] = bu[:, q * LANES:(q + 1) * LANES]
            bui[j * per + q] = bu[:, (per + q) * LANES:(per + q + 1) * LANES]

    def advance(t, prev):
        new = []
        for cb in range(n_tiles):
            lanes = slice(cb * LANES, (cb + 1) * LANES)
            ar, ai = ar_ref[:, lanes], ai_ref[:, lanes]
            pr, pi = prev[2 * cb], prev[2 * cb + 1]
            hr = ar * pr - ai * pi + bur[cb, tm(t), :]
            hi = ar * pi + ai * pr + bui[cb, tm(t), :]
            hre[cb, tm(t), :] = hr
            him[cb, tm(t), :] = hi
            new += [hr, hi]
        return tuple(new)

    state = tuple(ref[cb] for cb in range(n_tiles) for ref in (st_r, st_i))
    if n_steps <= 8:
        for t in range(n_steps):
            state = advance(t, state)
    else:
        state = lax.fori_loop(0, n_steps, advance, state)
    for cb in range(n_tiles):
        st_r[cb] = state[2 * cb]
        st_i[cb] = state[2 * cb + 1]
    if n_steps < tc:
        pad = pl.ds(n_steps * bsz, (tc - n_steps) * bsz)
        for cb in range(n_tiles):
            hre[cb, pad, :] = bur[cb, pad, :]
            him[cb, pad, :] = bui[cb, pad, :]

    def block(j):
        return jnp.concatenate([ref[j * per + q] for ref in (hre, him) for q in range(per)], axis=-1)

    u_tm = jnp.concatenate([wtm[q] for q in range(u_tiles)], axis=-1)
    y = jnp.concatenate([_dot(block(j), c_ref[j]) for j in range(A_BLOCKS)], axis=-1) + d_ref[...] * u_tm
    y = 0.5 * y * (1.0 + jnp.tanh(math.sqrt(2.0 / math.pi) * (y + 0.044715 * (y * y * y))))
    y = y * _sigmoid(_dot(y, gw_ref[...]) + gb_ref[...])
    for q in range(u_tiles):
        wtm[q] = y[:, q * LANES:(q + 1) * LANES]
    reorder(wtm, wbm, tm, bm)
    y_bm = jnp.concatenate([wbm[q] for q in range(u_tiles)], axis=-1)
    out = y_bm * _silu(za_ref[...].astype(F32).reshape(rows, A_WIDTH))
    out_ref[...] = out.reshape(bsz, tc, A_WIDTH).astype(out_ref.dtype)

    @pl.when(i == pl.num_programs(1) - 1)
    def _():
        for cb in range(n_tiles):
            hfr_ref[:, cb * LANES:(cb + 1) * LANES] = st_r[cb]
            hfi_ref[:, cb * LANES:(cb + 1) * LANES] = st_i[cb]


def _s5_mixer(proj, h0r, h0i, sp, *, bb, tc, t_valid):
    bsz, t, _ = proj.shape
    n_steps = tc if t_valid is None else t_valid
    full = lambda a: pl.BlockSpec(a.shape, lambda b, i: (0,) * a.ndim)
    col = lambda c: pl.BlockSpec((bb, tc, A_WIDTH), lambda b, i: (b, i, c))
    st = pl.BlockSpec((bb, A_LANES), lambda b, i: (b, 0))
    assert (PROJ_U, PROJ_ZA) == (0, 1)
    weights = [sp["ab_re"], sp["ab_im"], sp["bb"], sp["c"], sp["d"], sp["glu_w"], sp["glu_b"]]
    state = jax.ShapeDtypeStruct((bsz, A_LANES), F32)
    return pl.pallas_call(
        functools.partial(_s5_kernel, bsz=bb, tc=tc, n_steps=n_steps), grid=(bsz // bb, t // tc),
        in_specs=[col(0), col(1), st, st] + [full(a) for a in weights],
        out_specs=[col(0), st, st],
        out_shape=[jax.ShapeDtypeStruct((bsz, t, A_WIDTH), proj.dtype), state, state],
        scratch_shapes=[pltpu.VMEM((A_LANES // LANES, bb * tc, LANES), F32)] * 4
        + [pltpu.VMEM((A_LANES // LANES, bb, LANES), F32)] * 2
        + [pltpu.VMEM((A_WIDTH // LANES, bb * tc, LANES), F32)] * 2,
        compiler_params=_params(("parallel", "arbitrary")), name="s5_mixer",
    )(proj, proj, h0r, h0i, *weights)


def _hgrn_kernel(*refs, bb, chunk, n_chunks, t_valid, use_lb, slot, aliased):
    q_ref, f_ref, i_ref, zb_ref, lb_ref, nw_ref, s0_ref = refs[:7]
    out_ref, sfin_ref, s_scr = refs[-3:]
    c = chunk
    tb = pl.program_id(1)

    @pl.when(tb == 0)
    def _():
        s_scr[...] = jnp.swapaxes(s0_ref[0].reshape(bb * B_HEADS, B_DIM, B_DIM), 1, 2)

    incl, _, _ = _tri_masks(c)
    tri_f = incl.astype(F32)

    def body(ci, carry):
        off = pl.multiple_of(ci * c, c)
        sl = pl.ds(off, c)
        q = _silu(q_ref[:, sl, :].astype(F32))
        f = f_ref[:, sl, :]
        e = jnp.exp(-jnp.abs(f))
        r = 1.0 / (1.0 + e)
        k = jnp.where(f >= 0.0, e * r, r)
        if use_lb:
            lb = lb_ref[...]
            log_f = jnp.log(lb + (1.0 - lb) * jnp.where(f >= 0.0, r, e * r))
            k = (1.0 - lb) * k
        else:
            log_f = jnp.minimum(f, 0.0) - jnp.log1p(e)
        if t_valid is not None:
            tok = tb * (n_chunks * c) + off + lax.broadcasted_iota(jnp.int32, (1, c, 1), 1)
            log_f = jnp.where(tok < t_valid, log_f, 0.0)
            k = jnp.where(tok < t_valid, k, 0.0)
        cum = _cumsum_time(tri_f, log_f)
        last = cum[:, c - 1:c, :]
        mid = cum[:, c // 2 - 1:c // 2, :]
        rel = jnp.exp(cum - mid)
        q_mid = q * rel
        k_mid = k * (1.0 / rel)
        qm = _split_heads(q_mid, B_DIM)
        kh = _split_heads(k_mid, B_DIM)
        ke = _split_heads(k_mid * jnp.exp(last - mid), B_DIM)
        gm = _split_heads(jnp.exp(mid), B_DIM)
        vh = _split_heads(i_ref[:, sl, :].astype(F32), B_DIM)
        gl = _split_heads(jnp.exp(last), B_DIM)
        s = s_scr[...]
        both = _nt(qm, jnp.concatenate([s * gm, kh], axis=1))
        att = jnp.where(incl[None], both[:, :, B_DIM:], 0.0)
        o = _nn(att, vh) + both[:, :, :B_DIM]
        s_scr[...] = s * gl + _tn(vh, ke)
        o = o * lax.rsqrt(jnp.mean(o * o, axis=-1, keepdims=True) + RMS_EPS) * nw_ref[...]
        out_ref[:, sl, :] = (_merge_heads(o, bb) * _silu(zb_ref[:, sl, :].astype(F32))).astype(out_ref.dtype)
        return carry

    lax.fori_loop(0, n_chunks, body, 0)

    @pl.when(tb == pl.num_programs(1) - 1)
    def _():
        _write_state_slab(sfin_ref, slot, aliased,
                          jnp.swapaxes(s_scr[...], 1, 2).reshape(bb, B_HEADS, B_DIM, B_DIM))


def _hgrn_mixer(proj, proj_f, s0, lb, nw, prev, *, layer, n_layers, bb, tblock, chunk, t_valid, use_lb):
    bsz, t, _ = proj.shape
    st_out, st_shape, slot, aliased = _stacked_state_specs(prev, s0.shape[1:], n_layers, layer, bb)
    extra = ([pl.BlockSpec(memory_space=pl.ANY)], [prev], {7: 1}) if aliased else ([], [], {})
    col = lambda c: pl.BlockSpec((bb, tblock, B_WIDTH), lambda i, j: (i, j, c))
    vec = lambda a: pl.BlockSpec(a.shape, lambda i, j: (0, 0))
    st = pl.BlockSpec((1, bb, B_HEADS, B_DIM, B_DIM), lambda i, j: (layer, i, 0, 0, 0))
    return pl.pallas_call(
        functools.partial(_hgrn_kernel, bb=bb, chunk=chunk, n_chunks=tblock // chunk, t_valid=t_valid,
                          use_lb=use_lb, slot=slot, aliased=aliased),
        grid=(bsz // bb, t // tblock),
        in_specs=[col(PROJ_Q), col(0), col(PROJ_I), col(PROJ_ZB), vec(lb), vec(nw), st] + extra[0],
        out_specs=[pl.BlockSpec((bb, tblock, B_WIDTH), lambda i, j: (i, j, 0)), st_out],
        out_shape=[jax.ShapeDtypeStruct((bsz, t, B_WIDTH), proj.dtype), st_shape],
        scratch_shapes=[pltpu.VMEM((bb * B_HEADS, B_DIM, B_DIM), F32)],
        input_output_aliases=extra[2],
        compiler_params=_params(("parallel", "arbitrary")), name="hgrn2_mixer",
    )(proj, proj_f, proj, proj, lb, nw, s0, *extra[1])


def _rwkv_in_kernel(*refs, tblock, last_row, vres):
    h_ref, nw_ref, sh_ref, mix_ref, wp_ref, w0, w1, w2, a0, a1, a2, g1, g2 = refs[:13]
    if vres:
        v0, v1, v2, vf_ref = refs[13:17]
    r_ref, k_ref, v_ref, gate_ref, lw_ref, al_ref, last_ref, carry = refs[-8:]

    @pl.when(pl.program_id(1) == 0)
    def _():
        carry[...] = sh_ref[...]

    xn = _rms(h_ref[...], nw_ref[...])
    row = lax.broadcasted_iota(jnp.int32, (1, tblock, 1), 1)
    xx = jnp.where(row == 0, carry[...], pltpu.roll(xn, 1, axis=1)) - xn
    carry[...] = xn[:, tblock - 1:tblock, :]
    last_ref[...] = xn[:, last_row:last_row + 1, :]
    shape = xn.shape
    xn = xn.reshape(-1, shape[-1])
    xx = xx.reshape(-1, shape[-1])
    mixed = lambda m: xn + xx * mix_ref[m:m + 1, :]
    store = lambda ref, val: ref.__setitem__(Ellipsis, val.reshape(shape).astype(ref.dtype))
    x_v, x_g = mixed(3), mixed(5)
    store(r_ref, _dot(mixed(0), wp_ref[0]))
    store(k_ref, _dot(mixed(2), wp_ref[1]))
    v = _dot(x_v, wp_ref[2])
    if vres:
        v_first = vf_ref[...].reshape(v.shape).astype(F32)
        v = v + (v_first - v) * _sigmoid(v0[...] + _dot(_dot(x_v, v1[...]), v2[...]))
    store(v_ref, v)
    store(gate_ref, _dot(_sigmoid(_dot(x_g, g1[...])), g2[...]) * _silu(_dot(x_g, wp_ref[3])))
    store(lw_ref, -DECAY_SCALE * _sigmoid(w0[...] + _dot(jnp.tanh(_dot(mixed(1), w1[...])), w2[...])))
    store(al_ref, _sigmoid(a0[...] + _dot(_dot(mixed(4), a1[...]), a2[...])))


def _rwkv_in(h, nw, shift, mix, wp, lora_w, v_first, *, bb, tblock, t_valid, act_dtype):
    bsz, t, d = h.shape
    vres = v_first is not None
    tok = pl.BlockSpec((bb, tblock, d), lambda i, j: (i, j, 0))
    one = pl.BlockSpec((bb, 1, d), lambda i, j: (i, 0, 0))
    full = lambda a: pl.BlockSpec(a.shape, lambda i, j: (0,) * a.ndim)
    in_specs = [tok, full(nw), one, full(mix), full(wp)] + [full(a) for a in lora_w]
    args = [h, nw, shift.reshape(bsz, 1, d), mix, wp, *lora_w]
    if vres:
        in_specs.append(tok)
        args.append(v_first)
    act = lambda dt: jax.ShapeDtypeStruct((bsz, t, d), dt)
    last_row = (tblock if t_valid is None else t_valid) - 1
    return pl.pallas_call(
        functools.partial(_rwkv_in_kernel, tblock=tblock, last_row=last_row, vres=vres),
        grid=(bsz // bb, t // tblock),
        in_specs=in_specs, out_specs=[tok] * 6 + [one],
        out_shape=[act(act_dtype)] * 4 + [act(F32)] * 2 + [jax.ShapeDtypeStruct((bsz, 1, d), F32)],
        scratch_shapes=[pltpu.VMEM((bb, 1, d), F32)],
        compiler_params=_params(("parallel", "arbitrary")), name="rwkv_in_proj",
    )(*args)


def _rwkv_kernel(*refs, bb, chunk, n_chunks, t_valid, slot, aliased):
    r_ref, k_ref, v_ref, gate_ref, lw_ref, al_ref, kk_ref, ka_ref, rk_ref, lnw_ref, lnb_ref, seg_ref, s0_ref = refs[:13]
    y_ref, sfin_ref, s_scr = refs[-3:]
    c = chunk
    gw = C_PACK * C_HEAD
    aw = C_PACK * c
    n_groups = D_MODEL // gw
    n_sq = int(math.log2(c)) - 1
    tb = pl.program_id(1)

    @pl.when(tb == 0)
    def _():
        for b in range(bb):
            s_scr[b] = jnp.concatenate([s0_ref[b, h] for h in range(C_HEADS)], axis=-1)

    row = lax.broadcasted_iota(jnp.int32, (c, aw), 0)
    col = lax.broadcasted_iota(jnp.int32, (c, aw), 1) % c
    incl, strict = row >= col, row > col
    eye = (row == col).astype(F32)
    tri_f = _tri_masks(c)[0].astype(F32)

    def bd_mask(rows_per_head, lanes, lanes_per_head):
        r = lax.broadcasted_iota(jnp.int32, (C_PACK * rows_per_head, lanes), 0) // rows_per_head
        l = lax.broadcasted_iota(jnp.int32, (C_PACK * rows_per_head, lanes), 1) // lanes_per_head
        return r == l

    bd_c, bd_v, bd_p = bd_mask(c, gw, C_HEAD), bd_mask(C_HEAD, gw, C_HEAD), bd_mask(c, aw, c)
    lane_head = lax.broadcasted_iota(jnp.int32, (1, gw), 1) // C_HEAD
    seg = seg_ref[...]

    def bd(y, mask):
        return jnp.where(mask, jnp.tile(y, (C_PACK, 1)), 0.0)

    def seg_sum(xs, pieces=1):
        rows = xs[0].shape[0]
        per_dot = max(1, SEG_ROWS // (rows * pieces))
        result = []
        for start in range(0, len(xs), per_dot):
            parts = []
            for x in xs[start:start + per_dot]:
                for _ in range(pieces):
                    parts.append(x.astype(BF16).astype(F32))
                    x = x - parts[-1]
            out = jnp.dot(jnp.concatenate(parts, axis=0), seg, preferred_element_type=F32)
            blocks = [out[i * rows:(i + 1) * rows] for i in range(len(parts))]
            result += [sum(blocks[i:i + pieces]) for i in range(0, len(parts), pieces)]
        return result

    def nn(a, b):
        return jnp.dot(a, b, preferred_element_type=F32)

    def nt(a, b):
        return lax.dot_general(a, b, (((1,), (1,)), ((), ())), preferred_element_type=F32)

    def tn(a, b):
        return lax.dot_general(a, b, (((0,), (0,)), ((), ())), preferred_element_type=F32)

    def cat(a, b):
        return jnp.concatenate([a, b], axis=0)

    def body(ci, carry):
        off = pl.multiple_of(ci * c, c)
        sl = pl.ds(off, c)
        inst = [(b, slice(g * gw, (g + 1) * gw)) for b in range(bb) for g in range(n_groups)]
        r_l, k_l, v_l, kk_l, al_l, rk_l, gi_l, gv_l, gm_l = ([] for _ in range(9))
        for b in range(bb):
            r = r_ref[b, sl, :].astype(F32)
            k = k_ref[b, sl, :].astype(F32)
            v = v_ref[b, sl, :].astype(F32)
            lw = lw_ref[b, sl, :]
            al = al_ref[b, sl, :]
            kk_raw = k * kk_ref[...]
            k = k * (1.0 + (al - 1.0) * ka_ref[...])
            rk = r * k * rk_ref[...]
            if t_valid is not None:
                ok = tb * (n_chunks * c) + off + lax.broadcasted_iota(jnp.int32, (c, 1), 0) < t_valid
                lw = jnp.where(ok, lw, 0.0)
                k = jnp.where(ok, k, 0.0)
                al = jnp.where(ok, al, 0.0)
            cum = _cumsum_time(tri_f, lw[None])[0]
            g_in, g_inv, g_m = jnp.exp(cum), jnp.exp(-cum), jnp.exp(cum - lw)
            for g in range(n_groups):
                ln = slice(g * gw, (g + 1) * gw)
                for lst, val in ((r_l, r), (k_l, k), (v_l, v), (kk_l, kk_raw), (al_l, al), (rk_l, rk),
                                 (gi_l, g_in), (gv_l, g_inv), (gm_l, g_m)):
                    lst.append(val[:, ln])
        n = len(inst)
        each = range(n)
        ss_l = seg_sum([kk_l[i] * kk_l[i] for i in each], pieces=2)
        kk_l = [kk_l[i] / jnp.maximum(jnp.sqrt(ss_l[i]), 1e-12) for i in each]
        kt_l = [k_l[i] * gv_l[i] for i in each]
        bt_l = [kk_l[i] * al_l[i] * gv_l[i] for i in each]
        x_l = [cat(-kk_l[i] * gm_l[i], r_l[i] * gi_l[i]) for i in each]
        s_l = [s_scr[b, :, ln] for b, ln in inst]
        a_all = [nt(x_l[i], jnp.concatenate([bd(bt_l[i], bd_c), bd(kt_l[i], bd_c), bd(s_l[i], bd_v)], axis=0))
                 for i in each]
        a_b = [a[:, :aw] for a in a_all]
        a_k = [a[:, aw:2 * aw] for a in a_all]
        x_s = [a[:, 2 * aw:] for a in a_all]
        n_ab = [jnp.where(strict, a_b[i][:c], 0.0) for i in each]
        a_rb = [jnp.where(incl, a_b[i][c:], 0.0) for i in each]
        a_kk = [cat(jnp.where(strict, a_k[i][:c], 0.0), jnp.where(incl, a_k[i][c:], 0.0)) for i in each]
        a_v = [nn(a_kk[i], bd(v_l[i], bd_c)) for i in each]
        w_m = [x_s[i][:c] + a_v[i][:c] for i in each]
        t_m = [eye + n_ab[i] for i in each]
        p = n_ab
        if n_sq >= 1:
            p = [nn(p[i], bd(p[i], bd_p)) for i in each]
            for _ in range(n_sq - 1):
                both = [nn(cat(p[i], t_m[i]), bd(p[i], bd_p)) for i in each]
                p = [both[i][:c] for i in each]
                t_m = [t_m[i] + both[i][c:] for i in each]
            t_m = [t_m[i] + nn(t_m[i], bd(p[i], bd_p)) for i in each]
        u = [nn(t_m[i], bd(w_m[i], bd_c)) for i in each]
        y = [x_s[i][c:] + a_v[i][c:] for i in each]
        y_u = [nn(a_rb[i], bd(u[i], bd_c)) for i in each]
        full = [tn(cat(v_l[i], u[i]), cat(kt_l[i], bt_l[i])) for i in each]
        for i, (b, ln) in enumerate(inst):
            upd = sum(jnp.where(lane_head == h, full[i][h * C_HEAD:(h + 1) * C_HEAD], 0.0) for h in range(C_PACK))
            s_scr[b, :, ln] = (s_l[i] + upd) * gi_l[i][c - 1:c]
        y = [y[i] + y_u[i] for i in each]
        sums = seg_sum([cat(y[i], rk_l[i]) for i in each])
        dev = [y[i] - sums[i][:c] * (1.0 / C_HEAD) for i in each]
        var = seg_sum([dev[i] * dev[i] for i in each])
        bonus = [sums[i][c:] * v_l[i] for i in each]
        yn = [dev[i] * lax.rsqrt(var[i] * (1.0 / C_HEAD) + GN_EPS) for i in each]
        for b in range(bb):
            mine = range(b * n_groups, (b + 1) * n_groups)
            yn_b = jnp.concatenate([yn[i] for i in mine], axis=-1)
            bonus_b = jnp.concatenate([bonus[i] for i in mine], axis=-1)
            y_ref[b, sl, :] = ((yn_b * lnw_ref[...] + lnb_ref[...] + bonus_b)
                               * gate_ref[b, sl, :].astype(F32)).astype(y_ref.dtype)
        return carry

    lax.fori_loop(0, n_chunks, body, 0)

    @pl.when(tb == pl.num_programs(1) - 1)
    def _():
        final = jnp.stack([jnp.stack([s_scr[b, :, h * C_HEAD:(h + 1) * C_HEAD] for h in range(C_HEADS)])
                           for b in range(bb)])
        _write_state_slab(sfin_ref, slot, aliased, final)


def _rwkv_mixer(r, k, v, gate, lw, al, vecs, s0, prev, *, layer, n_layers, bb, tblock, chunk, t_valid):
    bsz, t, d = r.shape
    st_out, st_shape, slot, aliased = _stacked_state_specs(prev, s0.shape, n_layers, layer, bb)
    tok = pl.BlockSpec((bb, tblock, d), lambda i, j: (i, j, 0))
    vec = pl.BlockSpec((1, d), lambda i, j: (0, 0))
    st = pl.BlockSpec((bb, C_HEADS, C_HEAD, C_HEAD), lambda i, j: (i, 0, 0, 0))
    seg = jnp.kron(jnp.eye(C_PACK, dtype=F32), jnp.ones((C_HEAD, C_HEAD), F32))
    in_specs = [tok] * 6 + [vec] * 5 + [pl.BlockSpec(seg.shape, lambda i, j: (0, 0)), st]
    args = [r, k, v, gate, lw, al, *vecs, seg, s0]
    aliases = {}
    if aliased:
        aliases = {len(args): 1}
        in_specs.append(pl.BlockSpec(memory_space=pl.ANY))
        args.append(prev)
    return pl.pallas_call(
        functools.partial(_rwkv_kernel, bb=bb, chunk=chunk, n_chunks=tblock // chunk, t_valid=t_valid,
                          slot=slot, aliased=aliased),
        grid=(bsz // bb, t // tblock),
        in_specs=in_specs, out_specs=[tok, st_out], out_shape=[jax.ShapeDtypeStruct((bsz, t, d), r.dtype), st_shape],
        input_output_aliases=aliases,
        scratch_shapes=[pltpu.VMEM((bb, C_HEAD, d), F32)],
        compiler_params=_params(("parallel", "arbitrary")), name="rwkv7_mixer",
    )(*args)


def _s5_params(w, j):
    lr = jnp.minimum(w["ssm_lambda_re"][j], -1e-4)
    li = w["ssm_lambda_im"][j]
    step = jnp.exp(w["ssm_log_step"][j])[:, None]
    mag = jnp.exp(lr * step)
    ab_re = mag * jnp.cos(li * step)
    ab_im = mag * jnp.sin(li * step)
    den = lr * lr + li * li
    nr = ab_re - 1.0
    cr = (nr * lr + ab_im * li) / den
    ci = (ab_im * lr - nr * li) / den
    b_re, b_im = w["ssm_b_re"][j], w["ssm_b_im"][j]
    bb_re = cr[..., None] * b_re - ci[..., None] * b_im
    bb_im = cr[..., None] * b_im + ci[..., None] * b_re
    gpb = A_GROUPS // A_BLOCKS
    eye = jnp.eye(gpb, dtype=F32)

    def pack_in(bb):
        bb = bb.reshape(A_BLOCKS, gpb, A_STATE, A_GROUP)
        return jnp.einsum("bgph,gk->bghkp", bb, eye).reshape(A_BLOCKS, gpb * A_GROUP, gpb * A_STATE)

    def pack_out(c):
        c = c.reshape(A_BLOCKS, gpb, A_GROUP, A_STATE)
        return jnp.einsum("bghp,gk->bgpkh", c, eye).reshape(A_BLOCKS, gpb * A_STATE, gpb * A_GROUP)

    return dict(ab_re=ab_re.reshape(1, A_LANES), ab_im=ab_im.reshape(1, A_LANES),
                bb=jnp.concatenate([pack_in(bb_re), pack_in(bb_im)], axis=-1).astype(BF16),
                c=jnp.concatenate([pack_out(w["ssm_c_re"][j]), -pack_out(w["ssm_c_im"][j])], axis=1).astype(BF16),
                d=w["ssm_d"][j].reshape(1, A_WIDTH), glu_w=w["ssm_glu_w"][j],
                glu_b=w["ssm_glu_b"][j].reshape(1, A_WIDTH))


def _trunk(x, ssm_re, ssm_im, hgrn, wkv, shift, w, prep, cfg):
    bsz, t, d = x.shape
    t_valid = cfg["t_valid"]
    m = bsz * t
    h = x.reshape(m, d)
    n_re, n_im, n_sh = [], [], []
    hg_all = wkv_all = None
    lbs = prep["lbs"]
    v_first = None
    y = None
    depth = w["norm_w"].shape[0]
    for layer in range(depth):
        j = layer // 2
        nw = w["norm_w"][layer].reshape(1, d)
        last = layer == depth - 1
        final_nw = w["final_norm_w"].reshape(1, d) if last else None
        if layer % 2 == 0:
            proj, proj_f = _norm_matmul(h, nw, prep["w_in"][j], tm=cfg["tm"] // 2, n_f32=B_WIDTH,
                                        act_dtype=cfg["act_dtype"])
            proj, proj_f = proj.reshape(bsz, t, -1), proj_f.reshape(bsz, t, -1)
            out_a, hr, hi = _s5_mixer(proj, ssm_re[j].reshape(bsz, A_LANES), ssm_im[j].reshape(bsz, A_LANES),
                                      prep["s5"][j], bb=cfg["s5_bb"], tc=cfg["s5_tc"], t_valid=t_valid)
            out_b, hg_all = _hgrn_mixer(proj, proj_f, hgrn, lbs[j].reshape(1, B_WIDTH),
                                        w["hgrn_norm_w"][j].reshape(1, B_DIM), hg_all, layer=j,
                                        n_layers=hgrn.shape[0], bb=cfg["hgrn_bb"], tblock=cfg["hgrn_tblock"],
                                        chunk=cfg["chunk"], t_valid=t_valid, use_lb=j > 0)
            res = _proj_residual(h, [out_a.reshape(m, A_WIDTH), out_b.reshape(m, B_WIDTH)], w["even_w_out"][j],
                                 final_nw, tm=cfg["tm"])
            n_re.append(hr.reshape(bsz, A_GROUPS, A_STATE))
            n_im.append(hi.reshape(bsz, A_GROUPS, A_STATE))
        else:
            row = lambda a: a.reshape(1, -1)
            lora_w = [row(w["rw_w0"][j]), w["rw_w1"][j], w["rw_w2"][j], row(w["rw_a0"][j]), w["rw_a1"][j],
                      w["rw_a2"][j], w["rw_g1"][j], w["rw_g2"][j]]
            if v_first is not None:
                lora_w += [row(w["rw_v0"][j - 1]), w["rw_v1"][j - 1], w["rw_v2"][j - 1]]
            r, k, v, gate, lw, al, x_last = _rwkv_in(h.reshape(bsz, t, d), nw, shift[j], w["rw_mix"][j],
                                                     w["rw_w_rkvz"][j], lora_w, v_first, bb=cfg["in_bb"],
                                                     tblock=cfg["in_tblock"], t_valid=t_valid,
                                                     act_dtype=cfg["act_dtype"])
            n_sh.append(x_last.reshape(bsz, d))
            if v_first is None:
                v_first = v
            vecs = [row(w[n][j]) for n in ("rw_k_k", "rw_k_a", "rw_r_k", "rw_ln_w", "rw_ln_b")]
            y_mix, wkv_all = _rwkv_mixer(r, k, v, gate, lw, al, vecs, wkv[j], wkv_all, layer=j,
                                         n_layers=wkv.shape[0], bb=cfg["bb"], tblock=cfg["tblock"],
                                         chunk=cfg["rwkv_chunk"], t_valid=t_valid)
            res = _proj_residual(h, [y_mix.reshape(m, d)], w["rw_w_o"][j], final_nw, tm=cfg["tm"])
        if last:
            h, y = res
        else:
            h = res[0]
    return (y.reshape(bsz, t, d), jnp.stack(n_re), jnp.stack(n_im), hg_all, wkv_all,
            jnp.stack(n_sh))


MATMUL_WEIGHTS = ("even_w_in", "even_w_out", "ssm_glu_w", "rw_w_rkvz", "rw_w1", "rw_w2", "rw_a1", "rw_a2",
                  "rw_v1", "rw_v2", "rw_g1", "rw_g2", "rw_w_o")
PROMPT_CFG = dict(t_valid=None, tm=1024, s5_bb=8, s5_tc=128, bb=2, tblock=256, chunk=32, rwkv_chunk=64,
                  hgrn_bb=8, hgrn_tblock=256, in_bb=1, in_tblock=512, act_dtype=BF16)
SAMPLE_PAD = 8
SAMPLE_CFG = dict(t_valid=4, tm=1024, s5_bb=32, s5_tc=SAMPLE_PAD, bb=8, tblock=SAMPLE_PAD, chunk=SAMPLE_PAD,
                  rwkv_chunk=SAMPLE_PAD, hgrn_bb=8, hgrn_tblock=SAMPLE_PAD, in_bb=32, in_tblock=SAMPLE_PAD,
                  act_dtype=F32)


def kernel(x_prompt, x_sample, state_ssm_re, state_ssm_im, state_hgrn, state_wkv, state_shift, norm_w, final_norm_w, even_w_in, even_w_out, ssm_lambda_re, ssm_lambda_im, ssm_log_step, ssm_b_re, ssm_b_im, ssm_c_re, ssm_c_im, ssm_d, ssm_glu_w, ssm_glu_b, hgrn_lower_bounds, hgrn_norm_w, rw_mix, rw_w_rkvz, rw_w0, rw_w1, rw_w2, rw_a0, rw_a1, rw_a2, rw_v0, rw_v1, rw_v2, rw_g1, rw_g2, rw_k_k, rw_k_a, rw_r_k, rw_ln_w, rw_ln_b, rw_w_o):
    w = dict(norm_w=norm_w, final_norm_w=final_norm_w, even_w_in=even_w_in, even_w_out=even_w_out,
             ssm_lambda_re=ssm_lambda_re, ssm_lambda_im=ssm_lambda_im, ssm_log_step=ssm_log_step,
             ssm_b_re=ssm_b_re, ssm_b_im=ssm_b_im, ssm_c_re=ssm_c_re, ssm_c_im=ssm_c_im, ssm_d=ssm_d,
             ssm_glu_w=ssm_glu_w, ssm_glu_b=ssm_glu_b, hgrn_lower_bounds=hgrn_lower_bounds,
             hgrn_norm_w=hgrn_norm_w, rw_mix=rw_mix, rw_w_rkvz=rw_w_rkvz, rw_w0=rw_w0, rw_w1=rw_w1,
             rw_w2=rw_w2, rw_a0=rw_a0, rw_a1=rw_a1, rw_a2=rw_a2, rw_v0=rw_v0, rw_v1=rw_v1, rw_v2=rw_v2,
             rw_g1=rw_g1, rw_g2=rw_g2, rw_k_k=rw_k_k, rw_k_a=rw_k_a, rw_r_k=rw_r_k,
             rw_ln_w=rw_ln_w, rw_ln_b=rw_ln_b, rw_w_o=rw_w_o)
    for name in MATMUL_WEIGHTS:
        w[name] = w[name].astype(BF16)
    n_even, n_odd = state_hgrn.shape[0], state_wkv.shape[0]
    lbs = jax.nn.softmax(hgrn_lower_bounds, axis=0)
    w_in = w["even_w_in"]
    f_lo, f_hi = 2 * A_WIDTH + B_WIDTH, 2 * A_WIDTH + 2 * B_WIDTH
    prep = dict(lbs=jnp.cumsum(lbs, axis=0) - lbs[0], s5=[_s5_params(w, j) for j in range(n_even)],
                w_in=jnp.concatenate([w_in[..., :f_lo], w_in[..., f_hi:], w_in[..., f_lo:f_hi]], axis=-1))
    bp, bs, ts = x_prompt.shape[0], x_sample.shape[0], x_sample.shape[1]
    assert ts == SAMPLE_CFG["t_valid"]
    z_re = jnp.zeros((n_even, bp, A_GROUPS, A_STATE), F32)
    z_hg = jnp.zeros((n_even, bp, B_HEADS, B_DIM, B_DIM), F32)
    z_wkv = jnp.zeros((n_odd, bp, C_HEADS, C_HEAD, C_HEAD), F32)
    z_sh = jnp.zeros((n_odd, bp, D_MODEL), F32)
    prompt = _trunk(x_prompt, z_re, z_re, z_hg, z_wkv, z_sh, w, prep, PROMPT_CFG)
    xs = jnp.pad(x_sample, ((0, 0), (0, SAMPLE_PAD - ts), (0, 0)))
    sample = _trunk(xs, state_ssm_re, state_ssm_im, state_hgrn, state_wkv, state_shift, w, prep, SAMPLE_CFG)
    y_sample = sample[0][:, :ts]
    return (prompt[0], y_sample, *prompt[1:], *sample[1:])
```
